```python
import math
import jax, jax.numpy as jnp
from jax import lax
import numpy as np

D_MODEL = 1024
BATCH = 16
SEQ = 2048
DEPTH = 1

CHUNK = 64
Q_BLOCK = 128
ROPE_THETA = 500000.0
ROPE_FRACTION = 4
LN_EPS = 1e-5
A_HEADS = 8
A_HEAD_DIM = 64
A_WIDTH = A_HEADS * A_HEAD_DIM
IDX_HEADS = 8
IDX_DIM = 32
TOPK_MAX = 256
B_HEADS = 4
B_HEAD_DIM = 64
B_WIDTH = B_HEADS * 2 * B_HEAD_DIM
MEM_LEN = 256
C_HEADS = 4
C_HEAD_DIM = 128
C_WIDTH = C_HEADS * C_HEAD_DIM
N_BRANCH = 3
DEEPNORM_ALPHA = (2.0 * DEPTH) ** 0.25
DEEPNORM_BETA = (8.0 * DEPTH) ** -0.25
IN_SPLITS = (
    A_HEADS * A_HEAD_DIM,
    A_HEAD_DIM,
    A_HEAD_DIM,
    A_WIDTH,
    IDX_HEADS * IDX_DIM,
    IDX_DIM,
    IDX_HEADS,
    B_WIDTH,
    B_WIDTH,
    B_WIDTH,
    B_WIDTH,
    C_WIDTH,
    C_WIDTH,
    N_BRANCH * D_MODEL,
)
VALUE_SPLITS = (2, 9)
IN_WIDTH = sum(IN_SPLITS)

kernel_name = 'hybrid_dsa_diffattn_memxattn_deepnorm'


def _split_cols(h, sizes):
    offs = np.cumsum(sizes)[:-1].tolist()
    return jnp.split(h, offs, axis=-1)


def _layer_norm(x, g, b):
    xf = x.astype(jnp.float32)
    mu = jnp.mean(xf, -1, keepdims=True)
    var = jnp.mean(jnp.square(xf - mu), -1, keepdims=True)
    y = (xf - mu) * lax.rsqrt(var + LN_EPS) * g.astype(jnp.float32) + b.astype(jnp.float32)
    return y.astype(x.dtype)


def _partial_rope(x, pos):
    d = x.shape[-1]
    r = d // ROPE_FRACTION
    half = r // 2
    inv = jnp.power(ROPE_THETA, -jnp.arange(half, dtype=jnp.float32) * (2.0 / r))
    ang = pos[:, None] * inv[None, :]
    cos = jnp.cos(ang)[:, None, :]
    sin = jnp.sin(ang)[:, None, :]
    xf = x.astype(jnp.float32)
    x1, x2, xp = xf[..., :half], xf[..., half:r], xf[..., r:]
    out = jnp.concatenate([x1 * cos - x2 * sin, x2 * cos + x1 * sin, xp], axis=-1)
    return out.astype(x.dtype)


def _chunk_limit(t):
    return (t // CHUNK + 1) * CHUNK


def _to_blocks(a, nb):
    a = a.reshape((a.shape[0], nb, Q_BLOCK) + a.shape[2:])
    return jnp.moveaxis(a, 1, 0)


def _from_blocks(a):
    a = jnp.moveaxis(a, 0, 1)
    return a.reshape((a.shape[0], a.shape[1] * a.shape[2]) + a.shape[3:])


def _dsa_attention(q, k, v, iq, ik, iw):
    seq = q.shape[1]
    nb = seq // Q_BLOCK
    n_sel = min(TOPK_MAX, seq // 4)
    key_pos = jnp.arange(seq)
    scale = A_HEAD_DIM ** -0.5

    def block(args):
        j, qb, iqb, iwb = args
        limit = _chunk_limit(j * Q_BLOCK + jnp.arange(Q_BLOCK))
        rel = jax.nn.relu(jnp.einsum('bqhd,bsd->bqhs', iqb, ik).astype(jnp.float32))
        score = jnp.einsum('bqh,bqhs->bqs', iwb.astype(jnp.float32), rel)
        score = jnp.where(key_pos[None, None, :] < limit[None, :, None], score, -jnp.inf)
        _, idx = lax.top_k(score, n_sel)
        ok = idx < limit[None, :, None]
        k_sel = jax.vmap(lambda kk, ii: kk[ii])(k, idx)
        v_sel = jax.vmap(lambda vv, ii: vv[ii])(v, idx)
        logits = jnp.einsum('bqhd,bqkd->bqhk', qb, k_sel).astype(jnp.float32) * scale
        logits = jnp.where(ok[:, :, None, :], logits, -jnp.inf)
        p = jax.nn.softmax(logits, axis=-1).astype(v.dtype)
        return jnp.einsum('bqhk,bqkd->bqhd', p, v_sel)

    out = lax.map(block, (jnp.arange(nb), _to_blocks(q, nb), _to_blocks(iq, nb), _to_blocks(iw, nb)))
    return _from_blocks(out)


def _diff_attention(q, k, v, lam):
    seq = q.shape[1]
    nb = seq // Q_BLOCK
    key_pos = jnp.arange(seq)
    scale = B_HEAD_DIM ** -0.5

    def block(args):
        j, qb = args
        limit = _chunk_limit(j * Q_BLOCK + jnp.arange(Q_BLOCK))
        logits = jnp.einsum('bqhcd,bshcd->bhcqs', qb, k).astype(jnp.float32) * scale
        logits = jnp.where(key_pos[None, :] < limit[:, None], logits, -jnp.inf)
        p = jax.nn.softmax(logits, axis=-1)
        a = (p[:, :, 0] - lam * p[:, :, 1]).astype(v.dtype)
        return jnp.einsum('bhqs,bshe->bqhe', a, v)

    out = lax.map(block, (jnp.arange(nb), _to_blocks(q, nb)))
    return _from_blocks(out)


def _memory_attention(q, mk, mv):
    logits = jnp.einsum('bqhd,bmhd->bhqm', q, mk).astype(jnp.float32) * (C_HEAD_DIM ** -0.5)
    p = jax.nn.softmax(logits, axis=-1).astype(mv.dtype)
    return jnp.einsum('bhqm,bmhd->bqhd', p, mv)


def setup_inputs(seed: int = 0) -> dict:
    key = jax.random.key(seed)
    ks = jax.random.split(key, 16)
    f32 = jnp.float32
    x = jax.random.normal(ks[0], (BATCH, SEQ, D_MODEL), f32)
    mem = jax.random.normal(ks[1], (BATCH, MEM_LEN, D_MODEL), f32)
    ln_in_g = 1.0 + 0.02 * jax.random.normal(ks[2], (D_MODEL,), f32)
    ln_in_b = 0.02 * jax.random.normal(ks[3], (D_MODEL,), f32)
    col_scale = jnp.concatenate([jnp.full((s,), DEEPNORM_BETA if i in VALUE_SPLITS else 1.0, f32)
                                 for i, s in enumerate(IN_SPLITS)])
    w_in = jax.random.normal(ks[4], (DEPTH, D_MODEL, IN_WIDTH), f32) * (D_MODEL ** -0.5) * col_scale
    kv_scale = jnp.concatenate([jnp.ones((C_WIDTH,), f32), jnp.full((C_WIDTH,), DEEPNORM_BETA, f32)])
    w_mem_kv = jax.random.normal(ks[5], (DEPTH, D_MODEL, 2 * C_WIDTH), f32) * (D_MODEL ** -0.5) * kv_scale
    diff_lambda = 0.1 * jax.random.normal(ks[6], (DEPTH, 4, B_HEAD_DIM), f32)
    diff_norm_g = 1.0 + 0.02 * jax.random.normal(ks[7], (DEPTH, 2 * B_HEAD_DIM), f32)
    w_proj_a = jax.random.normal(ks[8], (DEPTH, A_WIDTH, D_MODEL), f32) * (A_WIDTH ** -0.5) * DEEPNORM_BETA
    w_proj_b = jax.random.normal(ks[9], (DEPTH, B_WIDTH, D_MODEL), f32) * (B_WIDTH ** -0.5) * DEEPNORM_BETA
    w_proj_c = jax.random.normal(ks[10], (DEPTH, C_WIDTH, D_MODEL), f32) * (C_WIDTH ** -0.5) * DEEPNORM_BETA
    w_out = jax.random.normal(ks[11], (DEPTH, D_MODEL, D_MODEL), f32) * (D_MODEL ** -0.5) * DEEPNORM_BETA
    ln_g = 1.0 + 0.02 * jax.random.normal(ks[12], (DEPTH, D_MODEL), f32)
    ln_b = 0.02 * jax.random.normal(ks[13], (DEPTH, D_MODEL), f32)
    return {'x': x, 'mem': mem, 'ln_in_g': ln_in_g, 'ln_in_b': ln_in_b, 'w_in': w_in,
            'w_mem_kv': w_mem_kv, 'diff_lambda': diff_lambda, 'diff_norm_g': diff_norm_g,
            'w_proj_a': w_proj_a, 'w_proj_b': w_proj_b, 'w_proj_c': w_proj_c, 'w_out': w_out,
            'ln_g': ln_g, 'ln_b': ln_b}


def reference(x, mem, ln_in_g, ln_in_b, w_in, w_mem_kv, diff_lambda, diff_norm_g,
              w_proj_a, w_proj_b, w_proj_c, w_out, ln_g, ln_b):
    bsz, seq, _ = x.shape
    pos = jnp.arange(seq, dtype=jnp.float32)
    h = _layer_norm(x, ln_in_g, ln_in_b)
    for l in range(DEPTH):
        lam_init = 0.8 - 0.6 * math.exp(-0.3 * l)
        proj = h @ w_in[l]
        (a_q, a_k, a_v, a_gate, i_q, i_k, i_w, b_q, b_k, b_v, b_gate,
         c_q, c_gate, m_gate) = _split_cols(proj, IN_SPLITS)

        a_q = _partial_rope(a_q.reshape(bsz, seq, A_HEADS, A_HEAD_DIM), pos)
        a_k = _partial_rope(a_k[:, :, None, :], pos)[:, :, 0]
        i_q = _partial_rope(i_q.reshape(bsz, seq, IDX_HEADS, IDX_DIM), pos)
        i_k = _partial_rope(i_k[:, :, None, :], pos)[:, :, 0]
        i_w = i_w * ((IDX_HEADS * IDX_DIM) ** -0.5)
        o_a = _dsa_attention(a_q, a_k, a_v, i_q, i_k, i_w).reshape(bsz, seq, A_WIDTH)
        y_a = (o_a * jax.nn.silu(a_gate)) @ w_proj_a[l]

        b_q = _partial_rope(b_q.reshape(bsz, seq, 2 * B_HEADS, B_HEAD_DIM), pos)
        b_k = _partial_rope(b_k.reshape(bsz, seq, 2 * B_HEADS, B_HEAD_DIM), pos)
        b_q = b_q.reshape(bsz, seq, B_HEADS, 2, B_HEAD_DIM)
        b_k = b_k.reshape(bsz, seq, B_HEADS, 2, B_HEAD_DIM)
        b_v = b_v.reshape(bsz, seq, B_HEADS, 2 * B_HEAD_DIM)
        dl = diff_lambda[l].astype(jnp.float32)
        lam = jnp.exp(jnp.sum(dl[0] * dl[1])) - jnp.exp(jnp.sum(dl[2] * dl[3])) + lam_init
        o_b = _diff_attention(b_q, b_k, b_v, lam).astype(jnp.float32)
        o_b = (o_b * lax.rsqrt(jnp.mean(o_b * o_b, -1, keepdims=True) + LN_EPS)
               * diff_norm_g[l].astype(jnp.float32) * (1.0 - lam_init)).astype(h.dtype)
        y_b = (o_b.reshape(bsz, seq, B_WIDTH) * jax.nn.silu(b_gate)) @ w_proj_b[l]

        mk, mv = jnp.split(mem @ w_mem_kv[l], 2, axis=-1)
        mk = mk.reshape(bsz, MEM_LEN, C_HEADS, C_HEAD_DIM)
        mv = mv.reshape(bsz, MEM_LEN, C_HEADS, C_HEAD_DIM)
        o_c = _memory_attention(c_q.reshape(bsz, seq, C_HEADS, C_HEAD_DIM), mk, mv)
        y_c = (o_c.reshape(bsz, seq, C_WIDTH) * jax.nn.silu(c_gate)) @ w_proj_c[l]

        g = jax.nn.sigmoid(m_gate.reshape(bsz, seq, N_BRANCH, D_MODEL))
        merged = g[:, :, 0] * y_a + g[:, :, 1] * y_b + g[:, :, 2] * y_c
        h = _layer_norm(DEEPNORM_ALPHA * h + merged @ w_out[l], ln_g[l], ln_b[l])
    return h
```

```python
import functools
import math

import numpy as np
import jax
import jax.numpy as jnp
from jax import lax
from jax.experimental import pallas as pl
from jax.experimental.pallas import tpu as pltpu

D_MODEL = 1024
CHUNK = 64
ROPE_THETA = 500000.0
ROPE_FRACTION = 4
LN_EPS = 1e-5
A_HEADS = 8
A_HEAD_DIM = 64
A_WIDTH = A_HEADS * A_HEAD_DIM
IDX_HEADS = 8
IDX_DIM = 32
TOPK_MAX = 256
B_HEADS = 4
B_HEAD_DIM = 64
B_WIDTH = B_HEADS * 2 * B_HEAD_DIM
C_HEADS = 4
C_HEAD_DIM = 128
C_WIDTH = C_HEADS * C_HEAD_DIM
N_BRANCH = 3
DEPTH = 1
DEEPNORM_ALPHA = (2.0 * DEPTH) ** 0.25
IN_SPLITS = (A_WIDTH, A_HEAD_DIM, A_HEAD_DIM, A_WIDTH, IDX_HEADS * IDX_DIM, IDX_DIM, IDX_HEADS,
             B_WIDTH, B_WIDTH, B_WIDTH, B_WIDTH, C_WIDTH, C_WIDTH, N_BRANCH * D_MODEL)

LANES = 128
QB = 256
KB = 256
NEG = -1e30
INT_MIN = -2 ** 31
VMEM_LIMIT = 56 * 1024 * 1024

BF16 = jnp.bfloat16
F32 = jnp.float32
I32 = jnp.int32


def _nt_dot(a, b):
    return lax.dot_general(a, b, (((1,), (1,)), ((), ())), preferred_element_type=F32)


def _dot(a, b):
    return jnp.dot(a, b, preferred_element_type=F32)


def _layer_norm(x, g, b):
    mu = jnp.mean(x, axis=-1, keepdims=True)
    xc = x - mu
    var = jnp.mean(xc * xc, axis=-1, keepdims=True)
    return xc * lax.rsqrt(var + LN_EPS) * g + b


def _mem_kv_kernel(mem_ref, wkt_ref, wv_ref, mkt_ref, mv_ref):
    mb = mem_ref[0].astype(BF16)
    mkt_ref[0] = _nt_dot(wkt_ref[...], mb).astype(BF16)
    mv_ref[0] = _dot(mb, wv_ref[...]).astype(BF16)


def _mem_kv(mem, wkt, wv):
    bsz, mlen, d = mem.shape
    return pl.pallas_call(
        _mem_kv_kernel,
        grid=(bsz,),
        in_specs=[pl.BlockSpec((1, mlen, d), lambda b: (b, 0, 0)),
                  pl.BlockSpec((C_WIDTH, d), lambda b: (0, 0)),
                  pl.BlockSpec((d, C_WIDTH), lambda b: (0, 0))],
        out_specs=[pl.BlockSpec((1, C_WIDTH, mlen), lambda b: (b, 0, 0)),
                   pl.BlockSpec((1, mlen, C_WIDTH), lambda b: (b, 0, 0))],
        out_shape=[jax.ShapeDtypeStruct((bsz, C_WIDTH, mlen), BF16),
                   jax.ShapeDtypeStruct((bsz, mlen, C_WIDTH), BF16)],
        name="mem_kv",
    )(mem, wkt, wv)


_T_AQ = 0
_T_BQ = _T_AQ + A_WIDTH
_T_IQ = _T_BQ + B_WIDTH
_T_BV = _T_IQ + IDX_HEADS * IDX_DIM
_T_AV = _T_BV + B_WIDTH
_T_IW = _T_AV + A_HEAD_DIM
_T_ROWS = _T_IW + 16
_N_COLS = B_WIDTH + LANES


def _rope_rows64(r, cos, sin):
    x1, x2 = r[0:8], r[8:16]
    return jnp.concatenate([x1 * cos - x2 * sin, x2 * cos + x1 * sin, r[16:]], axis=0)


def _in_proj_kernel(x_ref, g_ref, b_ref, wn_ref, wt_ref, cq_ref, sq_ref, ci_ref, si_ref, nk_ref, mk_ref,
                    aqt_ref, bqt_ref, iqt_ref, iwt_ref, avt_ref, bvt_ref, bk_ref, ak_ref, ik_ref):
    tm = x_ref.shape[0]
    hb = _layer_norm(x_ref[...], g_ref[...], b_ref[...]).astype(BF16)

    kn = _dot(hb, wn_ref[...])
    c64, a64, b64 = nk_ref[0], nk_ref[1], nk_ref[2]
    for g in range(B_WIDTH // LANES):
        v = kn[:, g * LANES:(g + 1) * LANES]
        v = v * c64 + pltpu.roll(v, 8, 1) * a64 + pltpu.roll(v, LANES - 8, 1) * b64
        bk_ref[:, g * LANES:(g + 1) * LANES] = v.astype(BF16)
    v = kn[:, B_WIDTH:]
    v = (v * mk_ref[0] + pltpu.roll(v, 8, 1) * mk_ref[1] + pltpu.roll(v, LANES - 8, 1) * mk_ref[2]
         + pltpu.roll(v, 4, 1) * mk_ref[3] + pltpu.roll(v, LANES - 4, 1) * mk_ref[4])
    ak_ref[...] = v[:, :A_HEAD_DIM].astype(BF16)
    ik_ref[...] = v[:, A_HEAD_DIM:A_HEAD_DIM + IDX_DIM].astype(BF16)

    cos, sin = cq_ref[...], sq_ref[...]
    for hd in range(A_HEADS):
        r = _nt_dot(wt_ref[_T_AQ + 64 * hd:_T_AQ + 64 * (hd + 1), :], hb)
        aqt_ref[0, 64 * hd:64 * (hd + 1), :] = _rope_rows64(r, cos, sin).astype(BF16)
    zeros = jnp.zeros((B_HEAD_DIM, tm), BF16)
    for mp in range(2 * B_HEADS):
        r = _nt_dot(wt_ref[_T_BQ + 64 * mp:_T_BQ + 64 * (mp + 1), :], hb)
        r = _rope_rows64(r, cos, sin).astype(BF16)
        bqt_ref[0, 128 * mp:128 * (mp + 1), :] = jnp.concatenate([r, zeros] if mp % 2 == 0 else [zeros, r], axis=0)
    ci, si = ci_ref[...], si_ref[...]
    r = _nt_dot(wt_ref[_T_IQ:_T_IQ + IDX_HEADS * IDX_DIM, :], hb)
    parts = []
    for hd in range(IDX_HEADS):
        top = r[IDX_DIM * hd:IDX_DIM * hd + 8]
        parts.append(top * ci + pltpu.roll(top, 4, 0) * si)
        parts.append(r[IDX_DIM * hd + 8:IDX_DIM * (hd + 1)])
    iqt_ref[0] = jnp.concatenate(parts, axis=0).astype(BF16)
    tpk = tm // KB
    r = _nt_dot(wt_ref[_T_BV:_T_BV + B_WIDTH, :], hb).astype(BF16)
    for t in range(tpk):
        bvt_ref[0, t] = r[:, t * KB:(t + 1) * KB]
    r = _nt_dot(wt_ref[_T_AV:_T_AV + A_HEAD_DIM, :], hb).astype(BF16)
    for t in range(tpk):
        avt_ref[0, t] = r[:, t * KB:(t + 1) * KB]
    r = _nt_dot(wt_ref[_T_IW:_T_IW + 16, :], hb)
    iwt_ref[0] = r[0:IDX_HEADS]


def _in_proj(x2, ln_g, ln_b, wn, wt, tabs, bsz, seq, tm):
    d = x2.shape[1]
    spb = seq // tm
    tpk = tm // KB
    cq, sq, ci, si, nk, mk = tabs
    tok = lambda i: (i, 0)
    tokt = lambda i: (i // spb, 0, i % spb)
    const2 = lambda i: (0, 0)
    return pl.pallas_call(
        _in_proj_kernel,
        grid=(bsz * spb,),
        in_specs=[pl.BlockSpec((tm, d), tok),
                  pl.BlockSpec((1, d), const2), pl.BlockSpec((1, d), const2),
                  pl.BlockSpec(wn.shape, const2), pl.BlockSpec(wt.shape, const2),
                  pl.BlockSpec((8, tm), lambda i: (0, i % spb)), pl.BlockSpec((8, tm), lambda i: (0, i % spb)),
                  pl.BlockSpec((8, tm), lambda i: (0, i % spb)), pl.BlockSpec((8, tm), lambda i: (0, i % spb)),
                  pl.BlockSpec((3, tm, LANES), lambda i: (0, i % spb, 0)),
                  pl.BlockSpec((5, tm, LANES), lambda i: (0, i % spb, 0))],
        out_specs=[pl.BlockSpec((1, A_WIDTH, tm), tokt),
                   pl.BlockSpec((1, 2 * B_WIDTH, tm), tokt),
                   pl.BlockSpec((1, IDX_HEADS * IDX_DIM, tm), tokt),
                   pl.BlockSpec((1, IDX_HEADS, tm), tokt),
                   pl.BlockSpec((1, tpk, A_HEAD_DIM, KB), lambda i: (i // spb, i % spb, 0, 0)),
                   pl.BlockSpec((1, tpk, B_WIDTH, KB), lambda i: (i // spb, i % spb, 0, 0)),
                   pl.BlockSpec((tm, B_WIDTH), tok),
                   pl.BlockSpec((tm, A_HEAD_DIM), tok),
                   pl.BlockSpec((tm, IDX_DIM), tok)],
        out_shape=[jax.ShapeDtypeStruct((bsz, A_WIDTH, seq), BF16),
                   jax.ShapeDtypeStruct((bsz, 2 * B_WIDTH, seq), BF16),
                   jax.ShapeDtypeStruct((bsz, IDX_HEADS * IDX_DIM, seq), BF16),
                   jax.ShapeDtypeStruct((bsz, IDX_HEADS, seq), F32),
                   jax.ShapeDtypeStruct((bsz, seq // KB, A_HEAD_DIM, KB), BF16),
                   jax.ShapeDtypeStruct((bsz, seq // KB, B_WIDTH, KB), BF16),
                   jax.ShapeDtypeStruct((bsz * seq, B_WIDTH), BF16),
                   jax.ShapeDtypeStruct((bsz * seq, A_HEAD_DIM), BF16),
                   jax.ShapeDtypeStruct((bsz * seq, IDX_DIM), BF16)],
        compiler_params=pltpu.CompilerParams(vmem_limit_bytes=VMEM_LIMIT),
        name="in_proj",
    )(x2, ln_g, ln_b, wn, wt, cq, sq, ci, si, nk, mk)


def _query_limit(j):
    lane = lax.broadcasted_iota(I32, (1, QB), 1)
    return j * QB + (lane // CHUNK + 1) * CHUNK


def _key_index(off):
    return off + lax.broadcasted_iota(I32, (KB, QB), 0)


def _mixer_a_kernel(n_sel, aqt_ref, iqt_ref, iwt_ref, ak_ref, ik_ref, avt_ref, o_ref,
                    keys_scr, m_scr, l_scr, acc_scr):
    j = pl.program_id(1)
    nkb = j + 1
    limit = _query_limit(j)
    iw = iwt_ref[0]

    def score_block(kb, carry):
        off = pl.multiple_of(kb * KB, KB)
        ikb = ik_ref[0, pl.ds(off, KB), :]
        sc = jnp.zeros((KB, QB), F32)
        for hd in range(IDX_HEADS):
            z = _dot(ikb, iqt_ref[0, IDX_DIM * hd:IDX_DIM * (hd + 1), :])
            sc = sc + iw[hd:hd + 1, :] * jnp.maximum(z, 0.0)
        bits = pltpu.bitcast(sc, I32)
        key = bits ^ ((bits >> 31) & 0x7FFFFFFF)
        keys_scr[pl.ds(off, KB), :] = jnp.where(_key_index(off) < limit, key, INT_MIN)
        return carry

    lax.fori_loop(0, nkb, score_block, 0)

    def count(pred_fn):
        def body(kb, c):
            off = pl.multiple_of(kb * KB, KB)
            hit = pred_fn(keys_scr[pl.ds(off, KB), :])
            return c + jnp.sum(jnp.where(hit, 1.0, 0.0), axis=0, keepdims=True)
        return lax.fori_loop(0, nkb, body, jnp.zeros((1, QB), F32))

    def bit_step(it, carry):
        thr, cnt = carry
        cand = thr + lax.shift_left(jnp.int32(1), 31 - it)
        c = count(lambda blk: blk >= cand)
        take = c >= n_sel
        return jnp.where(take, cand, thr), jnp.where(take, c, cnt)

    thr0 = jnp.full((1, QB), INT_MIN, I32)
    cnt0 = jnp.zeros((1, QB), F32) + (nkb * KB).astype(F32)
    thr, cnt = lax.fori_loop(0, 32, bit_step, (thr0, cnt0))

    tied = (cnt > n_sel) & (thr > INT_MIN)

    @pl.when(jnp.max(jnp.where(tied, 1.0, 0.0)) > 0.0)
    def _():
        need = n_sel - count(lambda blk: blk > thr)
        thr_tied = jnp.where(tied, thr, INT_MIN)
        tri = (lax.broadcasted_iota(I32, (KB, KB), 0) >= lax.broadcasted_iota(I32, (KB, KB), 1)).astype(BF16)

        def body(kb, seen):
            off = pl.multiple_of(kb * KB, KB)
            blk = keys_scr[pl.ds(off, KB), :]
            eq = blk == thr_tied
            eqb = jnp.where(eq, 1.0, 0.0).astype(BF16)
            rank = _dot(tri, eqb) + seen
            keys_scr[pl.ds(off, KB), :] = jnp.where(eq & (rank > need), INT_MIN, blk)
            return seen + jnp.sum(eqb.astype(F32), axis=0, keepdims=True)

        lax.fori_loop(0, nkb, body, jnp.zeros((1, QB), F32))

    thr = jnp.maximum(thr, INT_MIN + 1)

    m_scr[...] = jnp.full(m_scr.shape, NEG, F32)
    l_scr[...] = jnp.zeros(l_scr.shape, F32)
    acc_scr[...] = jnp.zeros(acc_scr.shape, F32)

    def attend(kb, carry):
        off = pl.multiple_of(kb * KB, KB)
        sel = keys_scr[pl.ds(off, KB), :] >= thr
        akb = ak_ref[0, pl.ds(off, KB), :]
        avb = avt_ref[0, kb]
        for hd in range(A_HEADS):
            rows = slice(A_HEAD_DIM * hd, A_HEAD_DIM * (hd + 1))
            s = jnp.where(sel, _dot(akb, aqt_ref[0, rows, :]), NEG)
            m_old = m_scr[hd:hd + 1, :]
            m_new = jnp.maximum(m_old, jnp.max(s, axis=0, keepdims=True))
            alpha = jnp.exp(m_old - m_new)
            p = jnp.exp(s - m_new)
            l_scr[hd:hd + 1, :] = alpha * l_scr[hd:hd + 1, :] + jnp.sum(p, axis=0, keepdims=True)
            acc_scr[rows, :] = alpha * acc_scr[rows, :] + _dot(avb, p.astype(BF16))
            m_scr[hd:hd + 1, :] = m_new
        return carry

    lax.fori_loop(0, nkb, attend, 0)

    outs = []
    for hd in range(A_HEADS):
        rows = slice(A_HEAD_DIM * hd, A_HEAD_DIM * (hd + 1))
        outs.append(acc_scr[rows, :] / l_scr[hd:hd + 1, :])
    o_ref[0] = jnp.concatenate(outs, axis=0).T.astype(BF16)


def _mixer_a(aqt, iqt, iwt, ak, ik, avt, bsz, seq):
    nqb = seq // QB
    qblk = lambda b, j: (b, 0, j)
    full3 = lambda b, j: (b, 0, 0)
    n_sel = min(TOPK_MAX, seq // 4)
    return pl.pallas_call(
        functools.partial(_mixer_a_kernel, n_sel),
        grid=(bsz, nqb),
        in_specs=[pl.BlockSpec((1, A_WIDTH, QB), qblk),
                  pl.BlockSpec((1, IDX_HEADS * IDX_DIM, QB), qblk),
                  pl.BlockSpec((1, IDX_HEADS, QB), qblk),
                  pl.BlockSpec((1, seq, A_HEAD_DIM), full3),
                  pl.BlockSpec((1, seq, IDX_DIM), full3),
                  pl.BlockSpec((1, seq // KB, A_HEAD_DIM, KB), lambda b, j: (b, 0, 0, 0))],
        out_specs=pl.BlockSpec((1, QB, A_WIDTH), lambda b, j: (b, j, 0)),
        out_shape=jax.ShapeDtypeStruct((bsz, seq, A_WIDTH), BF16),
        scratch_shapes=[pltpu.VMEM((seq, QB), I32),
                        pltpu.VMEM((A_HEADS, QB), F32),
                        pltpu.VMEM((A_HEADS, QB), F32),
                        pltpu.VMEM((A_WIDTH, QB), F32)],
        compiler_params=pltpu.CompilerParams(vmem_limit_bytes=VMEM_LIMIT),
        name="mixer_a",
    )(aqt, iqt, iwt, ak, ik, avt)


def _mixer_b_kernel(lam_init, bqt_ref, bk_ref, bvt_ref, dl_ref, g_ref, o_ref, m_scr, l_scr, acc_scr):
    j = pl.program_id(1)
    limit = _query_limit(j)
    vdim = 2 * B_HEAD_DIM

    m_scr[...] = jnp.full(m_scr.shape, NEG, F32)
    l_scr[...] = jnp.zeros(l_scr.shape, F32)
    acc_scr[...] = jnp.zeros(acc_scr.shape, F32)

    def attend(kb, masked):
        off = pl.multiple_of(kb * KB, KB)
        vis = (_key_index(off) < limit) if masked else None
        for hd in range(B_HEADS):
            kpair = bk_ref[0, pl.ds(off, KB), vdim * hd:vdim * (hd + 1)]
            vb = bvt_ref[0, kb, vdim * hd:vdim * (hd + 1), :]
            for c in range(2):
                mp = 2 * hd + c
                s = _dot(kpair, bqt_ref[0, vdim * mp:vdim * (mp + 1), :])
                if masked:
                    s = jnp.where(vis, s, NEG)
                m_old = m_scr[mp:mp + 1, :]
                m_new = jnp.maximum(m_old, jnp.max(s, axis=0, keepdims=True))
                alpha = jnp.exp(m_old - m_new)
                p = jnp.exp(s - m_new)
                l_scr[mp:mp + 1, :] = alpha * l_scr[mp:mp + 1, :] + jnp.sum(p, axis=0, keepdims=True)
                arows = slice(vdim * mp, vdim * (mp + 1))
                acc_scr[arows, :] = alpha * acc_scr[arows, :] + _dot(vb, p.astype(BF16))
                m_scr[mp:mp + 1, :] = m_new

    def full_block(kb, carry):
        attend(kb, False)
        return carry

    lax.fori_loop(0, j, full_block, 0)
    attend(j, True)

    dl = dl_ref[...]
    lam = (jnp.exp(jnp.sum(dl[0:1] * dl[1:2], axis=1, keepdims=True))
           - jnp.exp(jnp.sum(dl[2:3] * dl[3:4], axis=1, keepdims=True)) + lam_init)
    gain = g_ref[...] * (1.0 - lam_init)
    outs = []
    for hd in range(B_HEADS):
        m1, m2 = 2 * hd, 2 * hd + 1
        o1 = acc_scr[vdim * m1:vdim * (m1 + 1), :] / l_scr[m1:m1 + 1, :]
        o2 = acc_scr[vdim * m2:vdim * (m2 + 1), :] / l_scr[m2:m2 + 1, :]
        o = o1 - lam * o2
        ms = jnp.mean(o * o, axis=0, keepdims=True)
        outs.append(o * lax.rsqrt(ms + LN_EPS) * gain)
    o_ref[0] = jnp.concatenate(outs, axis=0).T.astype(BF16)


def _mixer_b(bqt, bk, bvt, dl, gcol, lam_init, bsz, seq):
    nqb = seq // QB
    return pl.pallas_call(
        functools.partial(_mixer_b_kernel, lam_init),
        grid=(bsz, nqb),
        in_specs=[pl.BlockSpec((1, 2 * B_WIDTH, QB), lambda b, j: (b, 0, j)),
                  pl.BlockSpec((1, seq, B_WIDTH), lambda b, j: (b, 0, 0)),
                  pl.BlockSpec((1, seq // KB, B_WIDTH, KB), lambda b, j: (b, 0, 0, 0)),
                  pl.BlockSpec((4, B_HEAD_DIM), lambda b, j: (0, 0)),
                  pl.BlockSpec((2 * B_HEAD_DIM, 1), lambda b, j: (0, 0))],
        out_specs=pl.BlockSpec((1, QB, B_WIDTH), lambda b, j: (b, j, 0)),
        out_shape=jax.ShapeDtypeStruct((bsz, seq, B_WIDTH), BF16),
        scratch_shapes=[pltpu.VMEM((2 * B_HEADS, QB), F32),
                        pltpu.VMEM((2 * B_HEADS, QB), F32),
                        pltpu.VMEM((2 * B_HEADS * 2 * B_HEAD_DIM, QB), F32)],
        compiler_params=pltpu.CompilerParams(vmem_limit_bytes=VMEM_LIMIT),
        name="mixer_b",
    )(bqt, bk, bvt, dl, gcol)


def _silu(x):
    return x * jax.nn.sigmoid(x)


def _tail_kernel(x_ref, oa_ref, ob_ref, mkt_ref, mv_ref, lig_ref, lib_ref, wcq_ref, wg_ref, wm_ref,
                 wpa_ref, wpb_ref, wpc_ref, wo_ref, lg_ref, lb_ref, out_ref):
    h = _layer_norm(x_ref[...], lig_ref[...], lib_ref[...])
    hb = h.astype(BF16)

    cq = _dot(hb, wcq_ref[...]).astype(BF16)
    oc = []
    for hd in range(C_HEADS):
        cols = slice(C_HEAD_DIM * hd, C_HEAD_DIM * (hd + 1))
        s = _dot(cq[:, cols], mkt_ref[0, cols, :]) * (C_HEAD_DIM ** -0.5)
        p = jnp.exp(s - jnp.max(s, axis=-1, keepdims=True))
        p = p / jnp.sum(p, axis=-1, keepdims=True)
        oc.append(_dot(p.astype(BF16), mv_ref[0, :, cols]))
    o_c = jnp.concatenate(oc, axis=1)

    branches = ((oa_ref[0].astype(F32), wpa_ref), (ob_ref[0].astype(F32), wpb_ref), (o_c, wpc_ref))
    merged = None
    for n, (o, wp_ref) in enumerate(branches):
        gate = _dot(hb, wg_ref[:, A_WIDTH * n:A_WIDTH * (n + 1)])
        y = _dot((o * _silu(gate)).astype(BF16), wp_ref[...])
        mg = jax.nn.sigmoid(_dot(hb, wm_ref[:, D_MODEL * n:D_MODEL * (n + 1)]))
        merged = mg * y if merged is None else merged + mg * y
    z = DEEPNORM_ALPHA * h + _dot(merged.astype(BF16), wo_ref[...])
    out_ref[...] = _layer_norm(z, lg_ref[...], lb_ref[...])


def _tail(x2, o_a, o_b, mkt, mv, lig, lib, wcq, wg, wm, wpa, wpb, wpc, wo, lg, lb, bsz, seq, tm):
    d = x2.shape[1]
    spb = seq // tm
    mlen = mv.shape[1]
    tok = lambda i: (i, 0)
    tok3 = lambda i: (i // spb, i % spb, 0)
    bat3 = lambda i: (i // spb, 0, 0)
    const2 = lambda i: (0, 0)
    wspec = lambda w: pl.BlockSpec(w.shape, const2, pipeline_mode=pl.Buffered(1))
    vec = pl.BlockSpec((1, d), const2)
    return pl.pallas_call(
        _tail_kernel,
        grid=(bsz * spb,),
        in_specs=[pl.BlockSpec((tm, d), tok),
                  pl.BlockSpec((1, tm, A_WIDTH), tok3),
                  pl.BlockSpec((1, tm, B_WIDTH), tok3),
                  pl.BlockSpec((1, C_WIDTH, mlen), bat3),
                  pl.BlockSpec((1, mlen, C_WIDTH), bat3),
                  vec, vec, wspec(wcq), wspec(wg), wspec(wm), wspec(wpa), wspec(wpb), wspec(wpc), wspec(wo),
                  vec, vec],
        out_specs=pl.BlockSpec((tm, d), tok),
        out_shape=jax.ShapeDtypeStruct((bsz * seq, d), F32),
        compiler_params=pltpu.CompilerParams(vmem_limit_bytes=VMEM_LIMIT),
        name="tail",
    )(x2, o_a, o_b, mkt, mv, lig, lib, wcq, wg, wm, wpa, wpb, wpc, wo, lg, lb)


def _rope_tables(seq):
    pos = jnp.arange(seq, dtype=F32)

    def cos_sin(dim):
        r = dim // ROPE_FRACTION
        half = r // 2
        inv = jnp.power(ROPE_THETA, -jnp.arange(half, dtype=F32) * (2.0 / r))
        ang = pos[:, None] * inv[None, :]
        return jnp.cos(ang), jnp.sin(ang), half

    c64, s64, h64 = cos_sin(A_HEAD_DIM)
    c32, s32, h32 = cos_sin(IDX_DIM)
    cq, sq = c64.T, s64.T
    ci = jnp.concatenate([c32.T, c32.T], axis=0)
    si = jnp.concatenate([-s32.T, s32.T], axis=0)

    def natural(cos, sin, half, dim, width):
        zero = jnp.zeros((seq, dim - 2 * half), F32)
        c = jnp.concatenate([cos, cos, jnp.ones_like(zero)], axis=1)
        a = jnp.concatenate([jnp.zeros_like(sin), sin, zero], axis=1)
        b = jnp.concatenate([-sin, jnp.zeros_like(sin), zero], axis=1)
        rep = width // dim
        return [jnp.tile(t, (1, rep)) for t in (c, a, b)]

    nk = jnp.stack(natural(c64, s64, h64, A_HEAD_DIM, LANES))
    ca, aa, ba = natural(c64, s64, h64, A_HEAD_DIM, A_HEAD_DIM)
    cb, ab, bb = natural(c32, s32, h32, IDX_DIM, IDX_DIM)
    pad = LANES - A_HEAD_DIM - IDX_DIM
    z64, z32, zp = jnp.zeros((seq, A_HEAD_DIM), F32), jnp.zeros((seq, IDX_DIM), F32), jnp.zeros((seq, pad), F32)
    mk = jnp.stack([jnp.concatenate([ca, cb, zp], axis=1),
                    jnp.concatenate([aa, z32, zp], axis=1),
                    jnp.concatenate([ba, z32, zp], axis=1),
                    jnp.concatenate([z64, ab, zp], axis=1),
                    jnp.concatenate([z64, bb, zp], axis=1)])
    return cq, sq, ci, si, nk, mk


def kernel(x, mem, ln_in_g, ln_in_b, w_in, w_mem_kv, diff_lambda, diff_norm_g,
           w_proj_a, w_proj_b, w_proj_c, w_out, ln_g, ln_b):
    bsz, seq, d = x.shape
    assert d == D_MODEL and seq % QB == 0 and w_in.shape[0] == DEPTH == 1
    tm_proj = 512 if seq % 512 == 0 else QB
    tm_tail = 256
    lam_init = 0.8 - 0.6 * math.exp(-0.3 * 0)

    offs = np.cumsum((0,) + IN_SPLITS)
    w = [w_in[0][:, offs[i]:offs[i + 1]] for i in range(len(IN_SPLITS))]
    (w_aq, w_ak, w_av, w_ag, w_iq, w_ik, w_iw, w_bq, w_bk, w_bv, w_bg, w_cq, w_cg, w_mg) = w
    wt = jnp.concatenate([w_aq * (A_HEAD_DIM ** -0.5), w_bq * (B_HEAD_DIM ** -0.5), w_iq, w_bv, w_av,
                          w_iw * ((IDX_HEADS * IDX_DIM) ** -0.5), jnp.zeros((d, 16 - IDX_HEADS), F32)],
                         axis=1).T.astype(BF16)
    wn = jnp.concatenate([w_bk, w_ak, w_ik, jnp.zeros((d, LANES - A_HEAD_DIM - IDX_DIM), F32)],
                         axis=1).astype(BF16)
    wg = jnp.concatenate([w_ag, w_bg, w_cg], axis=1).astype(BF16)
    row = lambda v: v.reshape(1, -1).astype(F32)

    x2 = x.reshape(bsz * seq, d)
    tabs = _rope_tables(seq)
    mkt, mv = _mem_kv(mem, w_mem_kv[0][:, :C_WIDTH].T.astype(BF16), w_mem_kv[0][:, C_WIDTH:].astype(BF16))
    aqt, bqt, iqt, iwt, avt, bvt, bk, ak, ik = _in_proj(
        x2, row(ln_in_g), row(ln_in_b), wn, wt, tabs, bsz, seq, tm_proj)
    o_a = _mixer_a(aqt, iqt, iwt, ak.reshape(bsz, seq, A_HEAD_DIM), ik.reshape(bsz, seq, IDX_DIM), avt, bsz, seq)
    o_b = _mixer_b(bqt, bk.reshape(bsz, seq, B_WIDTH), bvt, diff_lambda[0].astype(F32),
                   diff_norm_g[0].reshape(-1, 1).astype(F32), lam_init, bsz, seq)
    out = _tail(x2, o_a, o_b, mkt, mv, row(ln_in_g), row(ln_in_b), w_cq.astype(BF16), wg, w_mg.astype(BF16),
                w_proj_a[0].astype(BF16), w_proj_b[0].astype(BF16), w_proj_c[0].astype(BF16),
                w_out[0].astype(BF16), row(ln_g[0]), row(ln_b[0]), bsz, seq, tm_tail)
    return out.reshape(bsz, seq, d)
```

```python
import functools
import math

import numpy as np
import jax
import jax.numpy as jnp
from jax import lax
from jax.experimental import pallas as pl
from jax.experimental.pallas import tpu as pltpu

D_MODEL = 1024
CHUNK = 64
ROPE_THETA = 500000.0
ROPE_FRACTION = 4
LN_EPS = 1e-5
A_HEADS = 8
A_HEAD_DIM = 64
A_WIDTH = A_HEADS * A_HEAD_DIM
IDX_HEADS = 8
IDX_DIM = 32
TOPK_MAX = 256
B_HEADS = 4
B_HEAD_DIM = 64
B_WIDTH = B_HEADS * 2 * B_HEAD_DIM
C_HEADS = 4
C_HEAD_DIM = 128
C_WIDTH = C_HEADS * C_HEAD_DIM
N_BRANCH = 3
DEPTH = 1
DEEPNORM_ALPHA = (2.0 * DEPTH) ** 0.25
IN_SPLITS = (A_WIDTH, A_HEAD_DIM, A_HEAD_DIM, A_WIDTH, IDX_HEADS * IDX_DIM, IDX_DIM, IDX_HEADS,
             B_WIDTH, B_WIDTH, B_WIDTH, B_WIDTH, C_WIDTH, C_WIDTH, N_BRANCH * D_MODEL)

LANES = 128
QB = 256
KB = 256
NEG = -1e30
INT_MIN = -2 ** 31
VMEM_LIMIT = 56 * 1024 * 1024

BF16 = jnp.bfloat16
F32 = jnp.float32
I32 = jnp.int32


def _nt_dot(a, b):
    return lax.dot_general(a, b, (((1,), (1,)), ((), ())), preferred_element_type=F32)


def _dot(a, b):
    return jnp.dot(a, b, preferred_element_type=F32)


def _layer_norm(x, g, b):
    mu = jnp.mean(x, axis=-1, keepdims=True)
    xc = x - mu
    var = jnp.mean(xc * xc, axis=-1, keepdims=True)
    return xc * lax.rsqrt(var + LN_EPS) * g + b


def _mem_kv_kernel(mem_ref, wkt_ref, wv_ref, mkt_ref, mv_ref):
    mb = mem_ref[0].astype(BF16)
    mkt_ref[0] = _nt_dot(wkt_ref[...], mb).astype(BF16)
    mv_ref[0] = _dot(mb, wv_ref[...]).astype(BF16)


def _mem_kv(mem, wkt, wv):
    bsz, mlen, d = mem.shape
    return pl.pallas_call(
        _mem_kv_kernel,
        grid=(bsz,),
        in_specs=[pl.BlockSpec((1, mlen, d), lambda b: (b, 0, 0)),
                  pl.BlockSpec((C_WIDTH, d), lambda b: (0, 0)),
                  pl.BlockSpec((d, C_WIDTH), lambda b: (0, 0))],
        out_specs=[pl.BlockSpec((1, C_WIDTH, mlen), lambda b: (b, 0, 0)),
                   pl.BlockSpec((1, mlen, C_WIDTH), lambda b: (b, 0, 0))],
        out_shape=[jax.ShapeDtypeStruct((bsz, C_WIDTH, mlen), BF16),
                   jax.ShapeDtypeStruct((bsz, mlen, C_WIDTH), BF16)],
        name="mem_kv",
    )(mem, wkt, wv)


_T_AQ = 0
_T_BQ = _T_AQ + A_WIDTH
_T_IQ = _T_BQ + B_WIDTH
_T_BV = _T_IQ + IDX_HEADS * IDX_DIM
_T_AV = _T_BV + B_WIDTH
_T_IW = _T_AV + A_HEAD_DIM
_T_ROWS = _T_IW + 16
_N_COLS = B_WIDTH + LANES
_ONES_ROWS = 16
_AV_ROWS = A_HEAD_DIM + _ONES_ROWS
_BV_ROWS = 2 * B_HEAD_DIM + _ONES_ROWS
LOG2E = math.log2(math.e)


def _rope_rows64(r, cos, sin):
    x1, x2 = r[0:8], r[8:16]
    return jnp.concatenate([x1 * cos - x2 * sin, x2 * cos + x1 * sin, r[16:]], axis=0)


def _in_proj_kernel(x_ref, g_ref, b_ref, wn_ref, wt_ref, cq_ref, sq_ref, ci_ref, si_ref, nk_ref, mk_ref,
                    aqt_ref, bqt_ref, iqt_ref, iwt_ref, avt_ref, bvt_ref, bk_ref, ak_ref, ik_ref):
    tm = x_ref.shape[0]
    hb = _layer_norm(x_ref[...], g_ref[...], b_ref[...]).astype(BF16)

    kn = _dot(hb, wn_ref[...])
    c64, a64, b64 = nk_ref[0], nk_ref[1], nk_ref[2]
    for g in range(B_WIDTH // LANES):
        v = kn[:, g * LANES:(g + 1) * LANES]
        v = v * c64 + pltpu.roll(v, 8, 1) * a64 + pltpu.roll(v, LANES - 8, 1) * b64
        bk_ref[:, g * LANES:(g + 1) * LANES] = v.astype(BF16)
    v = kn[:, B_WIDTH:]
    v = (v * mk_ref[0] + pltpu.roll(v, 8, 1) * mk_ref[1] + pltpu.roll(v, LANES - 8, 1) * mk_ref[2]
         + pltpu.roll(v, 4, 1) * mk_ref[3] + pltpu.roll(v, LANES - 4, 1) * mk_ref[4])
    ak_ref[...] = v[:, :A_HEAD_DIM].astype(BF16)
    ik_ref[...] = v[:, A_HEAD_DIM:A_HEAD_DIM + IDX_DIM].astype(BF16)

    cos, sin = cq_ref[...], sq_ref[...]
    for hd in range(A_HEADS):
        r = _nt_dot(wt_ref[_T_AQ + 64 * hd:_T_AQ + 64 * (hd + 1), :], hb)
        aqt_ref[0, 64 * hd:64 * (hd + 1), :] = _rope_rows64(r, cos, sin).astype(BF16)
    zeros = jnp.zeros((B_HEAD_DIM, tm), BF16)
    for mp in range(2 * B_HEADS):
        r = _nt_dot(wt_ref[_T_BQ + 64 * mp:_T_BQ + 64 * (mp + 1), :], hb)
        r = _rope_rows64(r, cos, sin).astype(BF16)
        bqt_ref[0, 128 * mp:128 * (mp + 1), :] = jnp.concatenate([r, zeros] if mp % 2 == 0 else [zeros, r], axis=0)
    ci, si = ci_ref[...], si_ref[...]
    r = _nt_dot(wt_ref[_T_IQ:_T_IQ + IDX_HEADS * IDX_DIM, :], hb)
    parts = []
    for hd in range(IDX_HEADS):
        top = r[IDX_DIM * hd:IDX_DIM * hd + 8]
        parts.append(top * ci + pltpu.roll(top, 4, 0) * si)
        parts.append(r[IDX_DIM * hd + 8:IDX_DIM * (hd + 1)])
    iqt_ref[0] = jnp.concatenate(parts, axis=0).astype(BF16)
    tpk = tm // KB
    ones = jnp.ones((_ONES_ROWS, tm), BF16)
    vdim = 2 * B_HEAD_DIM
    parts = []
    for hd in range(B_HEADS):
        parts.append(_nt_dot(wt_ref[_T_BV + vdim * hd:_T_BV + vdim * (hd + 1), :], hb).astype(BF16))
        parts.append(ones)
    r = jnp.concatenate(parts, axis=0)
    for t in range(tpk):
        bvt_ref[0, t] = r[:, t * KB:(t + 1) * KB]
    r = _nt_dot(wt_ref[_T_AV:_T_AV + A_HEAD_DIM, :], hb).astype(BF16)
    r = jnp.concatenate([r, ones], axis=0)
    for t in range(tpk):
        avt_ref[0, t] = r[:, t * KB:(t + 1) * KB]
    r = _nt_dot(wt_ref[_T_IW:_T_IW + 16, :], hb)
    iwt_ref[0] = r[0:IDX_HEADS]


def _in_proj(x2, ln_g, ln_b, wn, wt, tabs, bsz, seq, tm):
    d = x2.shape[1]
    spb = seq // tm
    tpk = tm // KB
    cq, sq, ci, si, nk, mk = tabs
    tok = lambda i: (i, 0)
    tokt = lambda i: (i // spb, 0, i % spb)
    const2 = lambda i: (0, 0)
    return pl.pallas_call(
        _in_proj_kernel,
        grid=(bsz * spb,),
        in_specs=[pl.BlockSpec((tm, d), tok),
                  pl.BlockSpec((1, d), const2), pl.BlockSpec((1, d), const2),
                  pl.BlockSpec(wn.shape, const2), pl.BlockSpec(wt.shape, const2),
                  pl.BlockSpec((8, tm), lambda i: (0, i % spb)), pl.BlockSpec((8, tm), lambda i: (0, i % spb)),
                  pl.BlockSpec((8, tm), lambda i: (0, i % spb)), pl.BlockSpec((8, tm), lambda i: (0, i % spb)),
                  pl.BlockSpec((3, tm, LANES), lambda i: (0, i % spb, 0)),
                  pl.BlockSpec((5, tm, LANES), lambda i: (0, i % spb, 0))],
        out_specs=[pl.BlockSpec((1, A_WIDTH, tm), tokt),
                   pl.BlockSpec((1, 2 * B_WIDTH, tm), tokt),
                   pl.BlockSpec((1, IDX_HEADS * IDX_DIM, tm), tokt),
                   pl.BlockSpec((1, IDX_HEADS, tm), tokt),
                   pl.BlockSpec((1, tpk, _AV_ROWS, KB), lambda i: (i // spb, i % spb, 0, 0)),
                   pl.BlockSpec((1, tpk, B_HEADS * _BV_ROWS, KB), lambda i: (i // spb, i % spb, 0, 0)),
                   pl.BlockSpec((tm, B_WIDTH), tok),
                   pl.BlockSpec((tm, A_HEAD_DIM), tok),
                   pl.BlockSpec((tm, IDX_DIM), tok)],
        out_shape=[jax.ShapeDtypeStruct((bsz, A_WIDTH, seq), BF16),
                   jax.ShapeDtypeStruct((bsz, 2 * B_WIDTH, seq), BF16),
                   jax.ShapeDtypeStruct((bsz, IDX_HEADS * IDX_DIM, seq), BF16),
                   jax.ShapeDtypeStruct((bsz, IDX_HEADS, seq), F32),
                   jax.ShapeDtypeStruct((bsz, seq // KB, _AV_ROWS, KB), BF16),
                   jax.ShapeDtypeStruct((bsz, seq // KB, B_HEADS * _BV_ROWS, KB), BF16),
                   jax.ShapeDtypeStruct((bsz * seq, B_WIDTH), BF16),
                   jax.ShapeDtypeStruct((bsz * seq, A_HEAD_DIM), BF16),
                   jax.ShapeDtypeStruct((bsz * seq, IDX_DIM), BF16)],
        compiler_params=pltpu.CompilerParams(vmem_limit_bytes=VMEM_LIMIT),
        name="in_proj",
    )(x2, ln_g, ln_b, wn, wt, cq, sq, ci, si, nk, mk)


def _query_limit(j):
    lane = lax.broadcasted_iota(I32, (1, QB), 1)
    return j * QB + (lane // CHUNK + 1) * CHUNK


def _key_index(off):
    return off + lax.broadcasted_iota(I32, (KB, QB), 0)


def _online_softmax(nkb, logits, values, vrows, s_bufs, mx_bufs, m_scr, acc_scr):
    def stage_logits(kb, buf):
        for mp, s in enumerate(logits(kb)):
            s_bufs[buf][mp] = s
            mx_bufs[buf][mp, 0:1, :] = jnp.max(s, axis=0, keepdims=True)

    def stage_values(kb, buf):
        for mp, vt in enumerate(values(kb)):
            rows = slice(vrows * mp, vrows * (mp + 1))
            m_old = m_scr[mp, 0:1, :]
            m_new = jnp.maximum(m_old, mx_bufs[buf][mp, 0:1, :])
            alpha = jnp.exp2(m_old - m_new)
            p = jnp.exp2(s_bufs[buf][mp] - m_new).astype(BF16)
            acc_scr[rows, :] = alpha * acc_scr[rows, :] + _dot(vt, p)
            m_scr[mp, 0:1, :] = m_new

    def pair(t, carry):
        kb = 2 * t
        stage_logits(kb + 1, 1)
        stage_values(kb, 0)
        stage_logits(kb + 2, 0)
        stage_values(kb + 1, 1)
        return carry

    last = nkb - 1
    stage_logits(0, 0)
    lax.fori_loop(0, last // 2, pair, 0)

    @pl.when(last % 2 == 1)
    def _():
        stage_logits(last, 1)
        stage_values(last - 1, 0)
        stage_values(last, 1)

    @pl.when(last % 2 == 0)
    def _():
        stage_values(last, 0)


def _mixer_a_kernel(n_sel, aqt_ref, iqt_ref, iwt_ref, ak_ref, ik_ref, avt_ref, o_ref,
                    keys_scr, s0_scr, s1_scr, mx0_scr, mx1_scr, m_scr, acc_scr):
    s_bufs, mx_bufs = (s0_scr, s1_scr), (mx0_scr, mx1_scr)
    j = pl.program_id(1)
    nkb = j + 1
    limit = _query_limit(j)
    iw = iwt_ref[0]

    def score_block(kb, carry):
        off = pl.multiple_of(kb * KB, KB)
        ikb = ik_ref[0, pl.ds(off, KB), :]
        sc = jnp.zeros((KB, QB), F32)
        for hd in range(IDX_HEADS):
            z = _dot(ikb, iqt_ref[0, IDX_DIM * hd:IDX_DIM * (hd + 1), :])
            sc = sc + iw[hd:hd + 1, :] * jnp.maximum(z, 0.0)
        bits = pltpu.bitcast(sc, I32)
        key = bits ^ ((bits >> 31) & 0x7FFFFFFF)
        keys_scr[pl.ds(off, KB), :] = jnp.where(_key_index(off) < limit, key, INT_MIN)
        return carry

    lax.fori_loop(0, nkb, score_block, 0)

    def count(pred_fn):
        def body(kb, c):
            off = pl.multiple_of(kb * KB, KB)
            hit = pred_fn(keys_scr[pl.ds(off, KB), :])
            return c + jnp.sum(jnp.where(hit, 1.0, 0.0), axis=0, keepdims=True)
        return lax.fori_loop(0, nkb, body, jnp.zeros((1, QB), F32))

    def bit_step(it, carry):
        thr, cnt = carry
        cand = thr + lax.shift_left(jnp.int32(1), 31 - it)
        c = count(lambda blk: blk >= cand)
        take = c >= n_sel
        return jnp.where(take, cand, thr), jnp.where(take, c, cnt)

    thr0 = jnp.full((1, QB), INT_MIN, I32)
    cnt0 = jnp.zeros((1, QB), F32) + (nkb * KB).astype(F32)
    thr, cnt = lax.fori_loop(0, 32, bit_step, (thr0, cnt0))

    tied = (cnt > n_sel) & (thr > INT_MIN)

    @pl.when(jnp.max(jnp.where(tied, 1.0, 0.0)) > 0.0)
    def _():
        need = n_sel - count(lambda blk: blk > thr)
        thr_tied = jnp.where(tied, thr, INT_MIN)
        tri = (lax.broadcasted_iota(I32, (KB, KB), 0) >= lax.broadcasted_iota(I32, (KB, KB), 1)).astype(BF16)

        def body(kb, seen):
            off = pl.multiple_of(kb * KB, KB)
            blk = keys_scr[pl.ds(off, KB), :]
            eq = blk == thr_tied
            eqb = jnp.where(eq, 1.0, 0.0).astype(BF16)
            rank = _dot(tri, eqb) + seen
            keys_scr[pl.ds(off, KB), :] = jnp.where(eq & (rank > need), INT_MIN, blk)
            return seen + jnp.sum(eqb.astype(F32), axis=0, keepdims=True)

        lax.fori_loop(0, nkb, body, jnp.zeros((1, QB), F32))

    thr = jnp.maximum(thr, INT_MIN + 1)

    m_scr[...] = jnp.full(m_scr.shape, NEG, F32)
    acc_scr[...] = jnp.zeros(acc_scr.shape, F32)

    def logits(kb):
        off = pl.multiple_of(kb * KB, KB)
        sel = keys_scr[pl.ds(off, KB), :] >= thr
        akb = ak_ref[0, pl.ds(off, KB), :]
        for hd in range(A_HEADS):
            s = _dot(akb, aqt_ref[0, A_HEAD_DIM * hd:A_HEAD_DIM * (hd + 1), :])
            yield jnp.where(sel, s, NEG)

    def values(kb):
        avb = avt_ref[0, kb]
        return [avb] * A_HEADS

    _online_softmax(nkb, logits, values, _AV_ROWS, s_bufs, mx_bufs, m_scr, acc_scr)

    outs = []
    for hd in range(A_HEADS):
        base = _AV_ROWS * hd
        outs.append(acc_scr[base:base + A_HEAD_DIM, :] / acc_scr[base + A_HEAD_DIM:base + A_HEAD_DIM + 1, :])
    o_ref[0] = jnp.concatenate(outs, axis=0).T.astype(BF16)


def _mixer_a(aqt, iqt, iwt, ak, ik, avt, bsz, seq):
    nqb = seq // QB
    qblk = lambda b, j: (b, 0, j)
    full3 = lambda b, j: (b, 0, 0)
    n_sel = min(TOPK_MAX, seq // 4)
    return pl.pallas_call(
        functools.partial(_mixer_a_kernel, n_sel),
        grid=(bsz, nqb),
        in_specs=[pl.BlockSpec((1, A_WIDTH, QB), qblk),
                  pl.BlockSpec((1, IDX_HEADS * IDX_DIM, QB), qblk),
                  pl.BlockSpec((1, IDX_HEADS, QB), qblk),
                  pl.BlockSpec((1, seq, A_HEAD_DIM), full3),
                  pl.BlockSpec((1, seq, IDX_DIM), full3),
                  pl.BlockSpec((1, seq // KB, _AV_ROWS, KB), lambda b, j: (b, 0, 0, 0))],
        out_specs=pl.BlockSpec((1, QB, A_WIDTH), lambda b, j: (b, j, 0)),
        out_shape=jax.ShapeDtypeStruct((bsz, seq, A_WIDTH), BF16),
        scratch_shapes=[pltpu.VMEM((seq, QB), I32),
                        pltpu.VMEM((A_HEADS, KB, QB), F32),
                        pltpu.VMEM((A_HEADS, KB, QB), F32),
                        pltpu.VMEM((A_HEADS, 8, QB), F32),
                        pltpu.VMEM((A_HEADS, 8, QB), F32),
                        pltpu.VMEM((A_HEADS, 8, QB), F32),
                        pltpu.VMEM((A_HEADS * _AV_ROWS, QB), F32)],
        compiler_params=pltpu.CompilerParams(vmem_limit_bytes=VMEM_LIMIT),
        name="mixer_a",
    )(aqt, iqt, iwt, ak, ik, avt)


def _mixer_b_kernel(lam_init, bqt_ref, bk_ref, bvt_ref, dl_ref, g_ref, o_ref,
                    s0_scr, s1_scr, mx0_scr, mx1_scr, m_scr, acc_scr):
    s_bufs, mx_bufs = (s0_scr, s1_scr), (mx0_scr, mx1_scr)
    j = pl.program_id(1)
    limit = _query_limit(j)
    vdim = 2 * B_HEAD_DIM
    nmaps = 2 * B_HEADS

    m_scr[...] = jnp.full(m_scr.shape, NEG, F32)
    acc_scr[...] = jnp.zeros(acc_scr.shape, F32)

    def logits(kb):
        off = pl.multiple_of(kb * KB, KB)
        vis = _key_index(off) < limit
        for mp in range(nmaps):
            kpair = bk_ref[0, pl.ds(off, KB), vdim * (mp // 2):vdim * (mp // 2 + 1)]
            s = _dot(kpair, bqt_ref[0, vdim * mp:vdim * (mp + 1), :])
            yield jnp.where(vis, s, NEG)

    def values(kb):
        return [bvt_ref[0, kb, _BV_ROWS * (mp // 2):_BV_ROWS * (mp // 2 + 1), :] for mp in range(nmaps)]

    _online_softmax(j + 1, logits, values, _BV_ROWS, s_bufs, mx_bufs, m_scr, acc_scr)

    dl = dl_ref[...]
    lam = (jnp.exp(jnp.sum(dl[0:1] * dl[1:2], axis=1, keepdims=True))
           - jnp.exp(jnp.sum(dl[2:3] * dl[3:4], axis=1, keepdims=True)) + lam_init)
    gain = g_ref[...] * (1.0 - lam_init)
    outs = []
    for hd in range(B_HEADS):
        b1, b2 = _BV_ROWS * 2 * hd, _BV_ROWS * (2 * hd + 1)
        o1 = acc_scr[b1:b1 + vdim, :] / acc_scr[b1 + vdim:b1 + vdim + 1, :]
        o2 = acc_scr[b2:b2 + vdim, :] / acc_scr[b2 + vdim:b2 + vdim + 1, :]
        o = o1 - lam * o2
        ms = jnp.mean(o * o, axis=0, keepdims=True)
        outs.append(o * lax.rsqrt(ms + LN_EPS) * gain)
    o_ref[0] = jnp.concatenate(outs, axis=0).T.astype(BF16)


def _mixer_b(bqt, bk, bvt, dl, gcol, lam_init, bsz, seq):
    nqb = seq // QB
    return pl.pallas_call(
        functools.partial(_mixer_b_kernel, lam_init),
        grid=(bsz, nqb),
        in_specs=[pl.BlockSpec((1, 2 * B_WIDTH, QB), lambda b, j: (b, 0, j)),
                  pl.BlockSpec((1, seq, B_WIDTH), lambda b, j: (b, 0, 0)),
                  pl.BlockSpec((1, seq // KB, B_HEADS * _BV_ROWS, KB), lambda b, j: (b, 0, 0, 0)),
                  pl.BlockSpec((4, B_HEAD_DIM), lambda b, j: (0, 0)),
                  pl.BlockSpec((2 * B_HEAD_DIM, 1), lambda b, j: (0, 0))],
        out_specs=pl.BlockSpec((1, QB, B_WIDTH), lambda b, j: (b, j, 0)),
        out_shape=jax.ShapeDtypeStruct((bsz, seq, B_WIDTH), BF16),
        scratch_shapes=[pltpu.VMEM((2 * B_HEADS, KB, QB), F32),
                        pltpu.VMEM((2 * B_HEADS, KB, QB), F32),
                        pltpu.VMEM((2 * B_HEADS, 8, QB), F32),
                        pltpu.VMEM((2 * B_HEADS, 8, QB), F32),
                        pltpu.VMEM((2 * B_HEADS, 8, QB), F32),
                        pltpu.VMEM((2 * B_HEADS * _BV_ROWS, QB), F32)],
        compiler_params=pltpu.CompilerParams(vmem_limit_bytes=VMEM_LIMIT),
        name="mixer_b",
    )(bqt, bk, bvt, dl, gcol)


def _silu(x):
    return x * jax.nn.sigmoid(x)


def _tail_kernel(x_ref, oa_ref, ob_ref, mkt_ref, mv_ref, lig_ref, lib_ref, wcq_ref, wg_ref, wm_ref,
                 wpa_ref, wpb_ref, wpc_ref, wo_ref, lg_ref, lb_ref, out_ref):
    h = _layer_norm(x_ref[...], lig_ref[...], lib_ref[...])
    hb = h.astype(BF16)

    cq = _dot(hb, wcq_ref[...]).astype(BF16)
    oc = []
    for hd in range(C_HEADS):
        cols = slice(C_HEAD_DIM * hd, C_HEAD_DIM * (hd + 1))
        s = _dot(cq[:, cols], mkt_ref[0, cols, :]) * (C_HEAD_DIM ** -0.5)
        p = jnp.exp(s - jnp.max(s, axis=-1, keepdims=True))
        p = p / jnp.sum(p, axis=-1, keepdims=True)
        oc.append(_dot(p.astype(BF16), mv_ref[0, :, cols]))
    o_c = jnp.concatenate(oc, axis=1)

    branches = ((oa_ref[0].astype(F32), wpa_ref), (ob_ref[0].astype(F32), wpb_ref), (o_c, wpc_ref))
    merged = None
    for n, (o, wp_ref) in enumerate(branches):
        gate = _dot(hb, wg_ref[:, A_WIDTH * n:A_WIDTH * (n + 1)])
        y = _dot((o * _silu(gate)).astype(BF16), wp_ref[...])
        mg = jax.nn.sigmoid(_dot(hb, wm_ref[:, D_MODEL * n:D_MODEL * (n + 1)]))
        merged = mg * y if merged is None else merged + mg * y
    z = DEEPNORM_ALPHA * h + _dot(merged.astype(BF16), wo_ref[...])
    out_ref[...] = _layer_norm(z, lg_ref[...], lb_ref[...])


def _tail(x2, o_a, o_b, mkt, mv, lig, lib, wcq, wg, wm, wpa, wpb, wpc, wo, lg, lb, bsz, seq, tm):
    d = x2.shape[1]
    spb = seq // tm
    mlen = mv.shape[1]
    tok = lambda i: (i, 0)
    tok3 = lambda i: (i // spb, i % spb, 0)
    bat3 = lambda i: (i // spb, 0, 0)
    const2 = lambda i: (0, 0)
    wspec = lambda w: pl.BlockSpec(w.shape, const2, pipeline_mode=pl.Buffered(1))
    vec = pl.BlockSpec((1, d), const2)
    return pl.pallas_call(
        _tail_kernel,
        grid=(bsz * spb,),
        in_specs=[pl.BlockSpec((tm, d), tok),
                  pl.BlockSpec((1, tm, A_WIDTH), tok3),
                  pl.BlockSpec((1, tm, B_WIDTH), tok3),
                  pl.BlockSpec((1, C_WIDTH, mlen), bat3),
                  pl.BlockSpec((1, mlen, C_WIDTH), bat3),
                  vec, vec, wspec(wcq), wspec(wg), wspec(wm), wspec(wpa), wspec(wpb), wspec(wpc), wspec(wo),
                  vec, vec],
        out_specs=pl.BlockSpec((tm, d), tok),
        out_shape=jax.ShapeDtypeStruct((bsz * seq, d), F32),
        compiler_params=pltpu.CompilerParams(vmem_limit_bytes=VMEM_LIMIT),
        name="tail",
    )(x2, o_a, o_b, mkt, mv, lig, lib, wcq, wg, wm, wpa, wpb, wpc, wo, lg, lb)


def _rope_tables(seq):
    pos = jnp.arange(seq, dtype=F32)

    def cos_sin(dim):
        r = dim // ROPE_FRACTION
        half = r // 2
        inv = jnp.power(ROPE_THETA, -jnp.arange(half, dtype=F32) * (2.0 / r))
        ang = pos[:, None] * inv[None, :]
        return jnp.cos(ang), jnp.sin(ang), half

    c64, s64, h64 = cos_sin(A_HEAD_DIM)
    c32, s32, h32 = cos_sin(IDX_DIM)
    cq, sq = c64.T, s64.T
    ci = jnp.concatenate([c32.T, c32.T], axis=0)
    si = jnp.concatenate([-s32.T, s32.T], axis=0)

    def natural(cos, sin, half, dim, width):
        zero = jnp.zeros((seq, dim - 2 * half), F32)
        c = jnp.concatenate([cos, cos, jnp.ones_like(zero)], axis=1)
        a = jnp.concatenate([jnp.zeros_like(sin), sin, zero], axis=1)
        b = jnp.concatenate([-sin, jnp.zeros_like(sin), zero], axis=1)
        rep = width // dim
        return [jnp.tile(t, (1, rep)) for t in (c, a, b)]

    nk = jnp.stack(natural(c64, s64, h64, A_HEAD_DIM, LANES))
    ca, aa, ba = natural(c64, s64, h64, A_HEAD_DIM, A_HEAD_DIM)
    cb, ab, bb = natural(c32, s32, h32, IDX_DIM, IDX_DIM)
    pad = LANES - A_HEAD_DIM - IDX_DIM
    z64, z32, zp = jnp.zeros((seq, A_HEAD_DIM), F32), jnp.zeros((seq, IDX_DIM), F32), jnp.zeros((seq, pad), F32)
    mk = jnp.stack([jnp.concatenate([ca, cb, zp], axis=1),
                    jnp.concatenate([aa, z32, zp], axis=1),
                    jnp.concatenate([ba, z32, zp], axis=1),
                    jnp.concatenate([z64, ab, zp], axis=1),
                    jnp.concatenate([z64, bb, zp], axis=1)])
    return cq, sq, ci, si, nk, mk


def kernel(x, mem, ln_in_g, ln_in_b, w_in, w_mem_kv, diff_lambda, diff_norm_g,
           w_proj_a, w_proj_b, w_proj_c, w_out, ln_g, ln_b):
    bsz, seq, d = x.shape
    assert d == D_MODEL and seq % QB == 0 and w_in.shape[0] == DEPTH == 1
    tm_proj = 512 if seq % 512 == 0 else QB
    tm_tail = 256
    lam_init = 0.8 - 0.6 * math.exp(-0.3 * 0)

    offs = np.cumsum((0,) + IN_SPLITS)
    w = [w_in[0][:, offs[i]:offs[i + 1]] for i in range(len(IN_SPLITS))]
    (w_aq, w_ak, w_av, w_ag, w_iq, w_ik, w_iw, w_bq, w_bk, w_bv, w_bg, w_cq, w_cg, w_mg) = w
    wt = jnp.concatenate([w_aq * (A_HEAD_DIM ** -0.5 * LOG2E), w_bq * (B_HEAD_DIM ** -0.5 * LOG2E), w_iq, w_bv, w_av,
                          w_iw * ((IDX_HEADS * IDX_DIM) ** -0.5), jnp.zeros((d, 16 - IDX_HEADS), F32)],
                         axis=1).T.astype(BF16)
    wn = jnp.concatenate([w_bk, w_ak, w_ik, jnp.zeros((d, LANES - A_HEAD_DIM - IDX_DIM), F32)],
                         axis=1).astype(BF16)
    wg = jnp.concatenate([w_ag, w_bg, w_cg], axis=1).astype(BF16)
    row = lambda v: v.reshape(1, -1).astype(F32)

    x2 = x.reshape(bsz * seq, d)
    tabs = _rope_tables(seq)
    mkt, mv = _mem_kv(mem, w_mem_kv[0][:, :C_WIDTH].T.astype(BF16), w_mem_kv[0][:, C_WIDTH:].astype(BF16))
    aqt, bqt, iqt, iwt, avt, bvt, bk, ak, ik = _in_proj(
        x2, row(ln_in_g), row(ln_in_b), wn, wt, tabs, bsz, seq, tm_proj)
    o_a = _mixer_a(aqt, iqt, iwt, ak.reshape(bsz, seq, A_HEAD_DIM), ik.reshape(bsz, seq, IDX_DIM), avt, bsz, seq)
    o_b = _mixer_b(bqt, bk.reshape(bsz, seq, B_WIDTH), bvt, diff_lambda[0].astype(F32),
                   diff_norm_g[0].reshape(-1, 1).astype(F32), lam_init, bsz, seq)
    out = _tail(x2, o_a, o_b, mkt, mv, row(ln_in_g), row(ln_in_b), w_cq.astype(BF16), wg, w_mg.astype(BF16),
                w_proj_a[0].astype(BF16), w_proj_b[0].astype(BF16), w_proj_c[0].astype(BF16),
                w_out[0].astype(BF16), row(ln_g[0]), row(ln_b[0]), bsz, seq, tm_tail)
    return out.reshape(bsz, seq, d)
```

```python
import functools
import math

import numpy as np
import jax
import jax.numpy as jnp
from jax import lax
from jax.experimental import pallas as pl
from jax.experimental.pallas import tpu as pltpu

D_MODEL = 1024
CHUNK = 64
ROPE_THETA = 500000.0
ROPE_FRACTION = 4
LN_EPS = 1e-5
A_HEADS = 8
A_HEAD_DIM = 64
A_WIDTH = A_HEADS * A_HEAD_DIM
IDX_HEADS = 8
IDX_DIM = 32
TOPK_MAX = 256
B_HEADS = 4
B_HEAD_DIM = 64
B_WIDTH = B_HEADS * 2 * B_HEAD_DIM
C_HEADS = 4
C_HEAD_DIM = 128
C_WIDTH = C_HEADS * C_HEAD_DIM
N_BRANCH = 3
DEPTH = 1
DEEPNORM_ALPHA = (2.0 * DEPTH) ** 0.25
IN_SPLITS = (A_WIDTH, A_HEAD_DIM, A_HEAD_DIM, A_WIDTH, IDX_HEADS * IDX_DIM, IDX_DIM, IDX_HEADS,
             B_WIDTH, B_WIDTH, B_WIDTH, B_WIDTH, C_WIDTH, C_WIDTH, N_BRANCH * D_MODEL)

LANES = 128
QB = 256
KB = 256
NEG = -1e30
INT_MIN = -2 ** 31
VMEM_LIMIT = 56 * 1024 * 1024

BF16 = jnp.bfloat16
F32 = jnp.float32
I32 = jnp.int32
I16 = jnp.int16
PACK16 = 16
HALF16 = 2 ** 15


def _nt_dot(a, b):
    return lax.dot_general(a, b, (((1,), (1,)), ((), ())), preferred_element_type=F32)


def _dot(a, b):
    return jnp.dot(a, b, preferred_element_type=F32)


def _layer_norm(x, g, b):
    mu = jnp.mean(x, axis=-1, keepdims=True)
    xc = x - mu
    var = jnp.mean(xc * xc, axis=-1, keepdims=True)
    return xc * lax.rsqrt(var + LN_EPS) * g + b


def _mem_kv_kernel(mem_ref, wkt_ref, wv_ref, mkt_ref, mv_ref):
    mb = mem_ref[0].astype(BF16)
    mkt_ref[0] = _nt_dot(wkt_ref[...], mb).astype(BF16)
    mv_ref[0] = _dot(mb, wv_ref[...]).astype(BF16)


def _mem_kv(mem, wkt, wv):
    bsz, mlen, d = mem.shape
    return pl.pallas_call(
        _mem_kv_kernel,
        grid=(bsz,),
        in_specs=[pl.BlockSpec((1, mlen, d), lambda b: (b, 0, 0)),
                  pl.BlockSpec((C_WIDTH, d), lambda b: (0, 0)),
                  pl.BlockSpec((d, C_WIDTH), lambda b: (0, 0))],
        out_specs=[pl.BlockSpec((1, C_WIDTH, mlen), lambda b: (b, 0, 0)),
                   pl.BlockSpec((1, mlen, C_WIDTH), lambda b: (b, 0, 0))],
        out_shape=[jax.ShapeDtypeStruct((bsz, C_WIDTH, mlen), BF16),
                   jax.ShapeDtypeStruct((bsz, mlen, C_WIDTH), BF16)],
        name="mem_kv",
    )(mem, wkt, wv)


_T_AQ = 0
_T_BQ = _T_AQ + A_WIDTH
_T_IQ = _T_BQ + B_WIDTH
_T_BV = _T_IQ + IDX_HEADS * IDX_DIM
_T_AV = _T_BV + B_WIDTH
_T_IW = _T_AV + A_HEAD_DIM
_T_ROWS = _T_IW + 16
_N_COLS = B_WIDTH + LANES
_ONES_ROWS = 16
_AV_ROWS = A_HEAD_DIM + _ONES_ROWS
_BV_ROWS = 2 * B_HEAD_DIM + _ONES_ROWS
LOG2E = math.log2(math.e)


def _rope_rows64(r, cos, sin):
    x1, x2 = r[0:8], r[8:16]
    return jnp.concatenate([x1 * cos - x2 * sin, x2 * cos + x1 * sin, r[16:]], axis=0)


def _in_proj_kernel(x_ref, g_ref, b_ref, wn_ref, wt_ref, cq_ref, sq_ref, ci_ref, si_ref, nk_ref, mk_ref,
                    aqt_ref, bqt_ref, iqt_ref, iwt_ref, avt_ref, bvt_ref, bk_ref, ak_ref, ik_ref, rt_scr):
    tm = x_ref.shape[0]
    hb = _layer_norm(x_ref[...], g_ref[...], b_ref[...]).astype(BF16)

    kn = _dot(hb, wn_ref[...])
    c64, a64, b64 = nk_ref[0], nk_ref[1], nk_ref[2]
    for g in range(B_WIDTH // LANES):
        v = kn[:, g * LANES:(g + 1) * LANES]
        v = v * c64 + pltpu.roll(v, 8, 1) * a64 + pltpu.roll(v, LANES - 8, 1) * b64
        bk_ref[:, g * LANES:(g + 1) * LANES] = v.astype(BF16)
    v = kn[:, B_WIDTH:]
    v = (v * mk_ref[0] + pltpu.roll(v, 8, 1) * mk_ref[1] + pltpu.roll(v, LANES - 8, 1) * mk_ref[2]
         + pltpu.roll(v, 4, 1) * mk_ref[3] + pltpu.roll(v, LANES - 4, 1) * mk_ref[4])
    ak_ref[...] = v[:, :A_HEAD_DIM].astype(BF16)
    ik_ref[...] = v[:, A_HEAD_DIM:A_HEAD_DIM + IDX_DIM].astype(BF16)

    rt_scr[...] = _nt_dot(wt_ref[...], hb)
    cos, sin = cq_ref[...], sq_ref[...]
    for hd in range(A_HEADS):
        r = rt_scr[_T_AQ + 64 * hd:_T_AQ + 64 * (hd + 1), :]
        aqt_ref[0, 64 * hd:64 * (hd + 1), :] = _rope_rows64(r, cos, sin).astype(BF16)
    zeros = jnp.zeros((B_HEAD_DIM, tm), BF16)
    for mp in range(2 * B_HEADS):
        r = _rope_rows64(rt_scr[_T_BQ + 64 * mp:_T_BQ + 64 * (mp + 1), :], cos, sin).astype(BF16)
        bqt_ref[0, 128 * mp:128 * (mp + 1), :] = jnp.concatenate([r, zeros] if mp % 2 == 0 else [zeros, r], axis=0)
    ci, si = ci_ref[...], si_ref[...]
    for hd in range(IDX_HEADS):
        base = _T_IQ + IDX_DIM * hd
        top = rt_scr[base:base + 8, :]
        r = jnp.concatenate([top * ci + pltpu.roll(top, 4, 0) * si, rt_scr[base + 8:base + IDX_DIM, :]], axis=0)
        iqt_ref[0, IDX_DIM * hd:IDX_DIM * (hd + 1), :] = r.astype(BF16)
    ones = jnp.ones((_ONES_ROWS, KB), BF16)
    vdim = 2 * B_HEAD_DIM
    for t in range(tm // KB):
        cols = slice(t * KB, (t + 1) * KB)
        for hd in range(B_HEADS):
            bvt_ref[0, t, _BV_ROWS * hd:_BV_ROWS * hd + vdim, :] = (
                rt_scr[_T_BV + vdim * hd:_T_BV + vdim * (hd + 1), cols].astype(BF16))
            bvt_ref[0, t, _BV_ROWS * hd + vdim:_BV_ROWS * (hd + 1), :] = ones
        avt_ref[0, t, 0:A_HEAD_DIM, :] = rt_scr[_T_AV:_T_AV + A_HEAD_DIM, cols].astype(BF16)
        avt_ref[0, t, A_HEAD_DIM:_AV_ROWS, :] = ones
    iwt_ref[0] = rt_scr[_T_IW:_T_IW + IDX_HEADS, :]


def _in_proj(x2, ln_g, ln_b, wn, wt, tabs, bsz, seq, tm):
    d = x2.shape[1]
    spb = seq // tm
    tpk = tm // KB
    cq, sq, ci, si, nk, mk = tabs
    tok = lambda i: (i, 0)
    tokt = lambda i: (i // spb, 0, i % spb)
    const2 = lambda i: (0, 0)
    return pl.pallas_call(
        _in_proj_kernel,
        grid=(bsz * spb,),
        in_specs=[pl.BlockSpec((tm, d), tok),
                  pl.BlockSpec((1, d), const2), pl.BlockSpec((1, d), const2),
                  pl.BlockSpec(wn.shape, const2), pl.BlockSpec(wt.shape, const2),
                  pl.BlockSpec((8, tm), lambda i: (0, i % spb)), pl.BlockSpec((8, tm), lambda i: (0, i % spb)),
                  pl.BlockSpec((8, tm), lambda i: (0, i % spb)), pl.BlockSpec((8, tm), lambda i: (0, i % spb)),
                  pl.BlockSpec((3, tm, LANES), lambda i: (0, i % spb, 0)),
                  pl.BlockSpec((5, tm, LANES), lambda i: (0, i % spb, 0))],
        out_specs=[pl.BlockSpec((1, A_WIDTH, tm), tokt),
                   pl.BlockSpec((1, 2 * B_WIDTH, tm), tokt),
                   pl.BlockSpec((1, IDX_HEADS * IDX_DIM, tm), tokt),
                   pl.BlockSpec((1, IDX_HEADS, tm), tokt),
                   pl.BlockSpec((1, tpk, _AV_ROWS, KB), lambda i: (i // spb, i % spb, 0, 0)),
                   pl.BlockSpec((1, tpk, B_HEADS * _BV_ROWS, KB), lambda i: (i // spb, i % spb, 0, 0)),
                   pl.BlockSpec((tm, B_WIDTH), tok),
                   pl.BlockSpec((tm, A_HEAD_DIM), tok),
                   pl.BlockSpec((tm, IDX_DIM), tok)],
        out_shape=[jax.ShapeDtypeStruct((bsz, A_WIDTH, seq), BF16),
                   jax.ShapeDtypeStruct((bsz, 2 * B_WIDTH, seq), BF16),
                   jax.ShapeDtypeStruct((bsz, IDX_HEADS * IDX_DIM, seq), BF16),
                   jax.ShapeDtypeStruct((bsz, IDX_HEADS, seq), F32),
                   jax.ShapeDtypeStruct((bsz, seq // KB, _AV_ROWS, KB), BF16),
                   jax.ShapeDtypeStruct((bsz, seq // KB, B_HEADS * _BV_ROWS, KB), BF16),
                   jax.ShapeDtypeStruct((bsz * seq, B_WIDTH), BF16),
                   jax.ShapeDtypeStruct((bsz * seq, A_HEAD_DIM), BF16),
                   jax.ShapeDtypeStruct((bsz * seq, IDX_DIM), BF16)],
        scratch_shapes=[pltpu.VMEM((_T_ROWS, tm), F32)],
        compiler_params=pltpu.CompilerParams(vmem_limit_bytes=VMEM_LIMIT),
        name="in_proj",
    )(x2, ln_g, ln_b, wn, wt, cq, sq, ci, si, nk, mk)


def _query_limit(j):
    lane = lax.broadcasted_iota(I32, (1, QB), 1)
    return j * QB + (lane // CHUNK + 1) * CHUNK


def _key_index(off):
    return off + lax.broadcasted_iota(I32, (KB, QB), 0)


def _online_softmax(nkb, logits, values, vrows, s_bufs, mx_bufs, m_scr, acc_scr):
    def stage_logits(kb, buf):
        for mp, s in enumerate(logits(kb)):
            s_bufs[buf][mp] = s
            mx_bufs[buf][mp, 0:1, :] = jnp.max(s, axis=0, keepdims=True)

    def stage_values(kb, buf):
        for mp, vt in enumerate(values(kb)):
            rows = slice(vrows * mp, vrows * (mp + 1))
            m_old = m_scr[mp, 0:1, :]
            m_new = jnp.maximum(m_old, mx_bufs[buf][mp, 0:1, :])
            alpha = jnp.exp2(m_old - m_new)
            p = jnp.exp2(s_bufs[buf][mp] - m_new).astype(BF16)
            acc_scr[rows, :] = alpha * acc_scr[rows, :] + _dot(vt, p)
            m_scr[mp, 0:1, :] = m_new

    def pair(t, carry):
        kb = 2 * t
        stage_logits(kb + 1, 1)
        stage_values(kb, 0)
        stage_logits(kb + 2, 0)
        stage_values(kb + 1, 1)
        return carry

    last = nkb - 1
    stage_logits(0, 0)
    lax.fori_loop(0, last // 2, pair, 0)

    @pl.when(last % 2 == 1)
    def _():
        stage_logits(last, 1)
        stage_values(last - 1, 0)
        stage_values(last, 1)

    @pl.when(last % 2 == 0)
    def _():
        stage_values(last, 0)


def _mixer_a_kernel(n_sel, nqb, aqt_ref, iqt_ref, iwt_ref, ak_ref, ik_ref, avt_ref, o_ref,
                    keys_scr, hi_scr, lo_scr, thr_scr, s0_scr, s1_scr, mx0_scr, mx1_scr, m_scr, acc_scr):
    s_bufs, mx_bufs = (s0_scr, s1_scr), (mx0_scr, mx1_scr)
    j = pl.program_id(1)
    nkb = j + 1
    limit = _query_limit(j)
    iw = iwt_ref[0]

    def score_block(kb, carry):
        off = pl.multiple_of(kb * KB, KB)
        ikb = ik_ref[0, pl.ds(off, KB), :]
        sc = jnp.zeros((KB, QB), F32)
        for hd in range(IDX_HEADS):
            z = _dot(ikb, iqt_ref[0, IDX_DIM * hd:IDX_DIM * (hd + 1), :])
            sc = sc + iw[hd:hd + 1, :] * jnp.maximum(z, 0.0)
        bits = pltpu.bitcast(sc, I32)
        key = bits ^ ((bits >> 31) & 0x7FFFFFFF)
        key = jnp.where(_key_index(off) < limit, key, INT_MIN)
        keys_scr[pl.ds(off, KB), :] = key
        hi_scr[pl.ds(off, KB), :] = (key >> 16).astype(I16)
        lo_scr[pl.ds(off, KB), :] = ((key & 0xFFFF) - HALF16).astype(I16)
        return carry

    lax.fori_loop(0, nkb, score_block, 0)

    def select_threshold(nblk):
        def count_ge(src_scr, cand):
            cand16 = cand.astype(I16)
            part = jnp.zeros((PACK16, QB), I16)
            for kb in range(nblk):
                hit = jnp.where(src_scr[kb * KB:(kb + 1) * KB, :] >= cand16, jnp.int16(1), jnp.int16(0))
                for r in range(KB // PACK16):
                    part = part + hit[PACK16 * r:PACK16 * (r + 1)]
            return jnp.sum(part.astype(I32), axis=0, keepdims=True)

        def radix16(src_scr, need, cnt0):
            def step(it, carry):
                thr, cnt = carry
                cand = thr + lax.shift_left(jnp.int32(1), 15 - it)
                c = count_ge(src_scr, cand)
                take = c >= need
                return jnp.where(take, cand, thr), jnp.where(take, c, cnt)
            return lax.fori_loop(0, 16, step, (jnp.full((1, QB), -HALF16, I32), cnt0))

        t_hi, cnt_hi = radix16(hi_scr, n_sel, jnp.full((1, QB), nblk * KB, I32))
        above = jnp.where(t_hi == HALF16 - 1, 0, count_ge(hi_scr, t_hi + 1))
        t_hi16 = t_hi.astype(I16)
        for kb in range(nblk):
            rows = slice(kb * KB, (kb + 1) * KB)
            lo_scr[rows, :] = jnp.where(hi_scr[rows, :] == t_hi16, lo_scr[rows, :], jnp.int16(-HALF16))
        t_lo, cnt_lo = radix16(lo_scr, n_sel - above, cnt_hi - above)
        thr_scr[0:1, :] = t_hi * (2 * HALF16) + (t_lo + HALF16)
        thr_scr[1:2, :] = above + cnt_lo

    for k in range(nqb):
        pl.when(j == k)(functools.partial(select_threshold, k + 1))
    thr = thr_scr[0:1, :]
    cnt = thr_scr[1:2, :]

    def count(pred_fn):
        def body(kb, c):
            off = pl.multiple_of(kb * KB, KB)
            hit = pred_fn(keys_scr[pl.ds(off, KB), :])
            return c + jnp.sum(jnp.where(hit, 1.0, 0.0), axis=0, keepdims=True)
        return lax.fori_loop(0, nkb, body, jnp.zeros((1, QB), F32))

    tied = (cnt > n_sel) & (thr > INT_MIN)

    @pl.when(jnp.max(jnp.where(tied, 1.0, 0.0)) > 0.0)
    def _():
        need = n_sel - count(lambda blk: blk > thr)
        thr_tied = jnp.where(tied, thr, INT_MIN)
        tri = (lax.broadcasted_iota(I32, (KB, KB), 0) >= lax.broadcasted_iota(I32, (KB, KB), 1)).astype(BF16)

        def body(kb, seen):
            off = pl.multiple_of(kb * KB, KB)
            blk = keys_scr[pl.ds(off, KB), :]
            eq = blk == thr_tied
            eqb = jnp.where(eq, 1.0, 0.0).astype(BF16)
            rank = _dot(tri, eqb) + seen
            keys_scr[pl.ds(off, KB), :] = jnp.where(eq & (rank > need), INT_MIN, blk)
            return seen + jnp.sum(eqb.astype(F32), axis=0, keepdims=True)

        lax.fori_loop(0, nkb, body, jnp.zeros((1, QB), F32))

    thr = jnp.maximum(thr, INT_MIN + 1)

    m_scr[...] = jnp.full(m_scr.shape, NEG, F32)
    acc_scr[...] = jnp.zeros(acc_scr.shape, F32)

    def logits(kb):
        off = pl.multiple_of(kb * KB, KB)
        sel = keys_scr[pl.ds(off, KB), :] >= thr
        akb = ak_ref[0, pl.ds(off, KB), :]
        for hd in range(A_HEADS):
            s = _dot(akb, aqt_ref[0, A_HEAD_DIM * hd:A_HEAD_DIM * (hd + 1), :])
            yield jnp.where(sel, s, NEG)

    def values(kb):
        avb = avt_ref[0, kb]
        return [avb] * A_HEADS

    _online_softmax(nkb, logits, values, _AV_ROWS, s_bufs, mx_bufs, m_scr, acc_scr)

    outs = []
    for hd in range(A_HEADS):
        base = _AV_ROWS * hd
        outs.append(acc_scr[base:base + A_HEAD_DIM, :] / acc_scr[base + A_HEAD_DIM:base + A_HEAD_DIM + 1, :])
    o_ref[0] = jnp.concatenate(outs, axis=0).T.astype(BF16)


def _mixer_a(aqt, iqt, iwt, ak, ik, avt, bsz, seq):
    nqb = seq // QB
    qblk = lambda b, j: (b, 0, j)
    full3 = lambda b, j: (b, 0, 0)
    n_sel = min(TOPK_MAX, seq // 4)
    return pl.pallas_call(
        functools.partial(_mixer_a_kernel, n_sel, nqb),
        grid=(bsz, nqb),
        in_specs=[pl.BlockSpec((1, A_WIDTH, QB), qblk),
                  pl.BlockSpec((1, IDX_HEADS * IDX_DIM, QB), qblk),
                  pl.BlockSpec((1, IDX_HEADS, QB), qblk),
                  pl.BlockSpec((1, seq, A_HEAD_DIM), full3),
                  pl.BlockSpec((1, seq, IDX_DIM), full3),
                  pl.BlockSpec((1, seq // KB, _AV_ROWS, KB), lambda b, j: (b, 0, 0, 0))],
        out_specs=pl.BlockSpec((1, QB, A_WIDTH), lambda b, j: (b, j, 0)),
        out_shape=jax.ShapeDtypeStruct((bsz, seq, A_WIDTH), BF16),
        scratch_shapes=[pltpu.VMEM((seq, QB), I32),
                        pltpu.VMEM((seq, QB), I16),
                        pltpu.VMEM((seq, QB), I16),
                        pltpu.VMEM((8, QB), I32),
                        pltpu.VMEM((A_HEADS, KB, QB), F32),
                        pltpu.VMEM((A_HEADS, KB, QB), F32),
                        pltpu.VMEM((A_HEADS, 8, QB), F32),
                        pltpu.VMEM((A_HEADS, 8, QB), F32),
                        pltpu.VMEM((A_HEADS, 8, QB), F32),
                        pltpu.VMEM((A_HEADS * _AV_ROWS, QB), F32)],
        compiler_params=pltpu.CompilerParams(vmem_limit_bytes=VMEM_LIMIT),
        name="mixer_a",
    )(aqt, iqt, iwt, ak, ik, avt)


def _mixer_b_kernel(lam_init, bqt_ref, bk_ref, bvt_ref, dl_ref, g_ref, o_ref,
                    s0_scr, s1_scr, mx0_scr, mx1_scr, m_scr, acc_scr):
    s_bufs, mx_bufs = (s0_scr, s1_scr), (mx0_scr, mx1_scr)
    j = pl.program_id(1)
    limit = _query_limit(j)
    vdim = 2 * B_HEAD_DIM
    nmaps = 2 * B_HEADS

    m_scr[...] = jnp.full(m_scr.shape, NEG, F32)
    acc_scr[...] = jnp.zeros(acc_scr.shape, F32)

    def logits(kb):
        off = pl.multiple_of(kb * KB, KB)
        vis = _key_index(off) < limit
        for mp in range(nmaps):
            kpair = bk_ref[0, pl.ds(off, KB), vdim * (mp // 2):vdim * (mp // 2 + 1)]
            s = _dot(kpair, bqt_ref[0, vdim * mp:vdim * (mp + 1), :])
            yield jnp.where(vis, s, NEG)

    def values(kb):
        return [bvt_ref[0, kb, _BV_ROWS * (mp // 2):_BV_ROWS * (mp // 2 + 1), :] for mp in range(nmaps)]

    _online_softmax(j + 1, logits, values, _BV_ROWS, s_bufs, mx_bufs, m_scr, acc_scr)

    dl = dl_ref[...]
    lam = (jnp.exp(jnp.sum(dl[0:1] * dl[1:2], axis=1, keepdims=True))
           - jnp.exp(jnp.sum(dl[2:3] * dl[3:4], axis=1, keepdims=True)) + lam_init)
    gain = g_ref[...] * (1.0 - lam_init)
    outs = []
    for hd in range(B_HEADS):
        b1, b2 = _BV_ROWS * 2 * hd, _BV_ROWS * (2 * hd + 1)
        o1 = acc_scr[b1:b1 + vdim, :] / acc_scr[b1 + vdim:b1 + vdim + 1, :]
        o2 = acc_scr[b2:b2 + vdim, :] / acc_scr[b2 + vdim:b2 + vdim + 1, :]
        o = o1 - lam * o2
        ms = jnp.mean(o * o, axis=0, keepdims=True)
        outs.append(o * lax.rsqrt(ms + LN_EPS) * gain)
    o_ref[0] = jnp.concatenate(outs, axis=0).T.astype(BF16)


def _mixer_b(bqt, bk, bvt, dl, gcol, lam_init, bsz, seq):
    nqb = seq // QB
    return pl.pallas_call(
        functools.partial(_mixer_b_kernel, lam_init),
        grid=(bsz, nqb),
        in_specs=[pl.BlockSpec((1, 2 * B_WIDTH, QB), lambda b, j: (b, 0, j)),
                  pl.BlockSpec((1, seq, B_WIDTH), lambda b, j: (b, 0, 0)),
                  pl.BlockSpec((1, seq // KB, B_HEADS * _BV_ROWS, KB), lambda b, j: (b, 0, 0, 0)),
                  pl.BlockSpec((4, B_HEAD_DIM), lambda b, j: (0, 0)),
                  pl.BlockSpec((2 * B_HEAD_DIM, 1), lambda b, j: (0, 0))],
        out_specs=pl.BlockSpec((1, QB, B_WIDTH), lambda b, j: (b, j, 0)),
        out_shape=jax.ShapeDtypeStruct((bsz, seq, B_WIDTH), BF16),
        scratch_shapes=[pltpu.VMEM((2 * B_HEADS, KB, QB), F32),
                        pltpu.VMEM((2 * B_HEADS, KB, QB), F32),
                        pltpu.VMEM((2 * B_HEADS, 8, QB), F32),
                        pltpu.VMEM((2 * B_HEADS, 8, QB), F32),
                        pltpu.VMEM((2 * B_HEADS, 8, QB), F32),
                        pltpu.VMEM((2 * B_HEADS * _BV_ROWS, QB), F32)],
        compiler_params=pltpu.CompilerParams(vmem_limit_bytes=VMEM_LIMIT),
        name="mixer_b",
    )(bqt, bk, bvt, dl, gcol)


def _silu(x):
    return x * jax.nn.sigmoid(x)


def _tail_kernel(x_ref, oa_ref, ob_ref, mkt_ref, mv_ref, lig_ref, lib_ref, wcq_ref, wg_ref, wm_ref,
                 wpa_ref, wpb_ref, wpc_ref, wo_ref, lg_ref, lb_ref, out_ref):
    h = _layer_norm(x_ref[...], lig_ref[...], lib_ref[...])
    hb = h.astype(BF16)

    cq = _dot(hb, wcq_ref[...]).astype(BF16)
    oc = []
    for hd in range(C_HEADS):
        cols = slice(C_HEAD_DIM * hd, C_HEAD_DIM * (hd + 1))
        s = _dot(cq[:, cols], mkt_ref[0, cols, :]) * (C_HEAD_DIM ** -0.5)
        p = jnp.exp(s - jnp.max(s, axis=-1, keepdims=True))
        p = p / jnp.sum(p, axis=-1, keepdims=True)
        oc.append(_dot(p.astype(BF16), mv_ref[0, :, cols]))
    o_c = jnp.concatenate(oc, axis=1)

    branches = ((oa_ref[0].astype(F32), wpa_ref), (ob_ref[0].astype(F32), wpb_ref), (o_c, wpc_ref))
    merged = None
    for n, (o, wp_ref) in enumerate(branches):
        gate = _dot(hb, wg_ref[:, A_WIDTH * n:A_WIDTH * (n + 1)])
        y = _dot((o * _silu(gate)).astype(BF16), wp_ref[...])
        mg = jax.nn.sigmoid(_dot(hb, wm_ref[:, D_MODEL * n:D_MODEL * (n + 1)]))
        merged = mg * y if merged is None else merged + mg * y
    z = DEEPNORM_ALPHA * h + _dot(merged.astype(BF16), wo_ref[...])
    out_ref[...] = _layer_norm(z, lg_ref[...], lb_ref[...])


def _tail(x2, o_a, o_b, mkt, mv, lig, lib, wcq, wg, wm, wpa, wpb, wpc, wo, lg, lb, bsz, seq, tm):
    d = x2.shape[1]
    spb = seq // tm
    mlen = mv.shape[1]
    tok = lambda i: (i, 0)
    tok3 = lambda i: (i // spb, i % spb, 0)
    bat3 = lambda i: (i // spb, 0, 0)
    const2 = lambda i: (0, 0)
    wspec = lambda w: pl.BlockSpec(w.shape, const2, pipeline_mode=pl.Buffered(1))
    vec = pl.BlockSpec((1, d), const2)
    return pl.pallas_call(
        _tail_kernel,
        grid=(bsz * spb,),
        in_specs=[pl.BlockSpec((tm, d), tok),
                  pl.BlockSpec((1, tm, A_WIDTH), tok3),
                  pl.BlockSpec((1, tm, B_WIDTH), tok3),
                  pl.BlockSpec((1, C_WIDTH, mlen), bat3),
                  pl.BlockSpec((1, mlen, C_WIDTH), bat3),
                  vec, vec, wspec(wcq), wspec(wg), wspec(wm), wspec(wpa), wspec(wpb), wspec(wpc), wspec(wo),
                  vec, vec],
        out_specs=pl.BlockSpec((tm, d), tok),
        out_shape=jax.ShapeDtypeStruct((bsz * seq, d), F32),
        compiler_params=pltpu.CompilerParams(vmem_limit_bytes=VMEM_LIMIT),
        name="tail",
    )(x2, o_a, o_b, mkt, mv, lig, lib, wcq, wg, wm, wpa, wpb, wpc, wo, lg, lb)


def _rope_tables(seq):
    pos = jnp.arange(seq, dtype=F32)

    def cos_sin(dim):
        r = dim // ROPE_FRACTION
        half = r // 2
        inv = jnp.power(ROPE_THETA, -jnp.arange(half, dtype=F32) * (2.0 / r))
        ang = pos[:, None] * inv[None, :]
        return jnp.cos(ang), jnp.sin(ang), half

    c64, s64, h64 = cos_sin(A_HEAD_DIM)
    c32, s32, h32 = cos_sin(IDX_DIM)
    cq, sq = c64.T, s64.T
    ci = jnp.concatenate([c32.T, c32.T], axis=0)
    si = jnp.concatenate([-s32.T, s32.T], axis=0)

    def natural(cos, sin, half, dim, width):
        zero = jnp.zeros((seq, dim - 2 * half), F32)
        c = jnp.concatenate([cos, cos, jnp.ones_like(zero)], axis=1)
        a = jnp.concatenate([jnp.zeros_like(sin), sin, zero], axis=1)
        b = jnp.concatenate([-sin, jnp.zeros_like(sin), zero], axis=1)
        rep = width // dim
        return [jnp.tile(t, (1, rep)) for t in (c, a, b)]

    nk = jnp.stack(natural(c64, s64, h64, A_HEAD_DIM, LANES))
    ca, aa, ba = natural(c64, s64, h64, A_HEAD_DIM, A_HEAD_DIM)
    cb, ab, bb = natural(c32, s32, h32, IDX_DIM, IDX_DIM)
    pad = LANES - A_HEAD_DIM - IDX_DIM
    z64, z32, zp = jnp.zeros((seq, A_HEAD_DIM), F32), jnp.zeros((seq, IDX_DIM), F32), jnp.zeros((seq, pad), F32)
    mk = jnp.stack([jnp.concatenate([ca, cb, zp], axis=1),
                    jnp.concatenate([aa, z32, zp], axis=1),
                    jnp.concatenate([ba, z32, zp], axis=1),
                    jnp.concatenate([z64, ab, zp], axis=1),
                    jnp.concatenate([z64, bb, zp], axis=1)])
    return cq, sq, ci, si, nk, mk


def kernel(x, mem, ln_in_g, ln_in_b, w_in, w_mem_kv, diff_lambda, diff_norm_g,
           w_proj_a, w_proj_b, w_proj_c, w_out, ln_g, ln_b):
    bsz, seq, d = x.shape
    assert d == D_MODEL and seq % QB == 0 and w_in.shape[0] == DEPTH == 1
    tm_proj = 512 if seq % 512 == 0 else QB
    tm_tail = 256
    lam_init = 0.8 - 0.6 * math.exp(-0.3 * 0)

    offs = np.cumsum((0,) + IN_SPLITS)
    w = [w_in[0][:, offs[i]:offs[i + 1]] for i in range(len(IN_SPLITS))]
    (w_aq, w_ak, w_av, w_ag, w_iq, w_ik, w_iw, w_bq, w_bk, w_bv, w_bg, w_cq, w_cg, w_mg) = w
    wt = jnp.concatenate([w_aq * (A_HEAD_DIM ** -0.5 * LOG2E), w_bq * (B_HEAD_DIM ** -0.5 * LOG2E), w_iq, w_bv, w_av,
                          w_iw * ((IDX_HEADS * IDX_DIM) ** -0.5), jnp.zeros((d, 16 - IDX_HEADS), F32)],
                         axis=1).T.astype(BF16)
    wn = jnp.concatenate([w_bk, w_ak, w_ik, jnp.zeros((d, LANES - A_HEAD_DIM - IDX_DIM), F32)],
                         axis=1).astype(BF16)
    wg = jnp.concatenate([w_ag, w_bg, w_cg], axis=1).astype(BF16)
    row = lambda v: v.reshape(1, -1).astype(F32)

    x2 = x.reshape(bsz * seq, d)
    tabs = _rope_tables(seq)
    mkt, mv = _mem_kv(mem, w_mem_kv[0][:, :C_WIDTH].T.astype(BF16), w_mem_kv[0][:, C_WIDTH:].astype(BF16))
    aqt, bqt, iqt, iwt, avt, bvt, bk, ak, ik = _in_proj(
        x2, row(ln_in_g), row(ln_in_b), wn, wt, tabs, bsz, seq, tm_proj)
    o_a = _mixer_a(aqt, iqt, iwt, ak.reshape(bsz, seq, A_HEAD_DIM), ik.reshape(bsz, seq, IDX_DIM), avt, bsz, seq)
    o_b = _mixer_b(bqt, bk.reshape(bsz, seq, B_WIDTH), bvt, diff_lambda[0].astype(F32),
                   diff_norm_g[0].reshape(-1, 1).astype(F32), lam_init, bsz, seq)
    out = _tail(x2, o_a, o_b, mkt, mv, row(ln_in_g), row(ln_in_b), w_cq.astype(BF16), wg, w_mg.astype(BF16),
                w_proj_a[0].astype(BF16), w_proj_b[0].astype(BF16), w_proj_c[0].astype(BF16),
                w_out[0].astype(BF16), row(ln_g[0]), row(ln_b[0]), bsz, seq, tm_tail)
    return out.reshape(bsz, seq, d)
```

```python
import functools
import math

import numpy as np
import jax
import jax.numpy as jnp
from jax import lax
from jax.experimental import pallas as pl
from jax.experimental.pallas import tpu as pltpu

D_MODEL = 1024
CHUNK = 64
ROPE_THETA = 500000.0
ROPE_FRACTION = 4
LN_EPS = 1e-5
A_HEADS = 8
A_HEAD_DIM = 64
A_WIDTH = A_HEADS * A_HEAD_DIM
IDX_HEADS = 8
IDX_DIM = 32
TOPK_MAX = 256
B_HEADS = 4
B_HEAD_DIM = 64
B_WIDTH = B_HEADS * 2 * B_HEAD_DIM
C_HEADS = 4
C_HEAD_DIM = 128
C_WIDTH = C_HEADS * C_HEAD_DIM
N_BRANCH = 3
DEPTH = 1
DEEPNORM_ALPHA = (2.0 * DEPTH) ** 0.25
IN_SPLITS = (A_WIDTH, A_HEAD_DIM, A_HEAD_DIM, A_WIDTH, IDX_HEADS * IDX_DIM, IDX_DIM, IDX_HEADS,
             B_WIDTH, B_WIDTH, B_WIDTH, B_WIDTH, C_WIDTH, C_WIDTH, N_BRANCH * D_MODEL)

LANES = 128
QB = 256
KB = 256
NEG = -1e30
INT_MIN = -2 ** 31
VMEM_LIMIT = 56 * 1024 * 1024

BF16 = jnp.bfloat16
F32 = jnp.float32
I32 = jnp.int32
I16 = jnp.int16
PACK16 = 16
HALF16 = 2 ** 15


def _nt_dot(a, b):
    return lax.dot_general(a, b, (((1,), (1,)), ((), ())), preferred_element_type=F32)


def _dot(a, b):
    return jnp.dot(a, b, preferred_element_type=F32)


def _layer_norm(x, g, b):
    mu = jnp.mean(x, axis=-1, keepdims=True)
    xc = x - mu
    var = jnp.mean(xc * xc, axis=-1, keepdims=True)
    return xc * lax.rsqrt(var + LN_EPS) * g + b


def _mem_kv_kernel(mem_ref, wkt_ref, wv_ref, mkt_ref, mv_ref):
    mb = mem_ref[0].astype(BF16)
    mkt_ref[0] = _nt_dot(wkt_ref[...], mb).astype(BF16)
    mv_ref[0] = _dot(mb, wv_ref[...]).astype(BF16)


def _mem_kv(mem, wkt, wv):
    bsz, mlen, d = mem.shape
    return pl.pallas_call(
        _mem_kv_kernel,
        grid=(bsz,),
        in_specs=[pl.BlockSpec((1, mlen, d), lambda b: (b, 0, 0)),
                  pl.BlockSpec((C_WIDTH, d), lambda b: (0, 0)),
                  pl.BlockSpec((d, C_WIDTH), lambda b: (0, 0))],
        out_specs=[pl.BlockSpec((1, C_WIDTH, mlen), lambda b: (b, 0, 0)),
                   pl.BlockSpec((1, mlen, C_WIDTH), lambda b: (b, 0, 0))],
        out_shape=[jax.ShapeDtypeStruct((bsz, C_WIDTH, mlen), BF16),
                   jax.ShapeDtypeStruct((bsz, mlen, C_WIDTH), BF16)],
        name="mem_kv",
    )(mem, wkt, wv)


_T_AQ = 0
_T_BQ = _T_AQ + A_WIDTH
_T_IQ = _T_BQ + B_WIDTH
_T_BV = _T_IQ + IDX_HEADS * IDX_DIM
_T_AV = _T_BV + B_WIDTH
_T_IW = _T_AV + A_HEAD_DIM
_T_ROWS = _T_IW + 16
_N_COLS = B_WIDTH + LANES
_ONES_ROWS = 16
_AV_ROWS = A_HEAD_DIM + _ONES_ROWS
_BV_ROWS = 2 * B_HEAD_DIM + _ONES_ROWS
LOG2E = math.log2(math.e)


def _rope_rows64(r, cos, sin):
    x1, x2 = r[0:8], r[8:16]
    return jnp.concatenate([x1 * cos - x2 * sin, x2 * cos + x1 * sin, r[16:]], axis=0)


def _in_proj_kernel(x_ref, g_ref, b_ref, wn_ref, wt_ref, cq_ref, sq_ref, ci_ref, si_ref, nk_ref, mk_ref,
                    aqt_ref, bqt_ref, iqt_ref, iwt_ref, avt_ref, bvt_ref, bk_ref, ak_ref, ik_ref, rt_scr):
    tm = x_ref.shape[0]
    hb = _layer_norm(x_ref[...], g_ref[...], b_ref[...]).astype(BF16)

    kn = _dot(hb, wn_ref[...])
    c64, a64, b64 = nk_ref[0], nk_ref[1], nk_ref[2]
    for g in range(B_WIDTH // LANES):
        v = kn[:, g * LANES:(g + 1) * LANES]
        v = v * c64 + pltpu.roll(v, 8, 1) * a64 + pltpu.roll(v, LANES - 8, 1) * b64
        bk_ref[:, g * LANES:(g + 1) * LANES] = v.astype(BF16)
    v = kn[:, B_WIDTH:]
    v = (v * mk_ref[0] + pltpu.roll(v, 8, 1) * mk_ref[1] + pltpu.roll(v, LANES - 8, 1) * mk_ref[2]
         + pltpu.roll(v, 4, 1) * mk_ref[3] + pltpu.roll(v, LANES - 4, 1) * mk_ref[4])
    ak_ref[...] = v[:, :A_HEAD_DIM].astype(BF16)
    ik_ref[...] = v[:, A_HEAD_DIM:A_HEAD_DIM + IDX_DIM].astype(BF16)

    rt_scr[...] = _nt_dot(wt_ref[...], hb)
    cos, sin = cq_ref[...], sq_ref[...]
    for hd in range(A_HEADS):
        r = rt_scr[_T_AQ + 64 * hd:_T_AQ + 64 * (hd + 1), :]
        aqt_ref[0, 64 * hd:64 * (hd + 1), :] = _rope_rows64(r, cos, sin).astype(BF16)
    zeros = jnp.zeros((B_HEAD_DIM, tm), BF16)
    for mp in range(2 * B_HEADS):
        r = _rope_rows64(rt_scr[_T_BQ + 64 * mp:_T_BQ + 64 * (mp + 1), :], cos, sin).astype(BF16)
        bqt_ref[0, 128 * mp:128 * (mp + 1), :] = jnp.concatenate([r, zeros] if mp % 2 == 0 else [zeros, r], axis=0)
    ci, si = ci_ref[...], si_ref[...]
    for hd in range(IDX_HEADS):
        base = _T_IQ + IDX_DIM * hd
        top = rt_scr[base:base + 8, :]
        r = jnp.concatenate([top * ci + pltpu.roll(top, 4, 0) * si, rt_scr[base + 8:base + IDX_DIM, :]], axis=0)
        iqt_ref[0, IDX_DIM * hd:IDX_DIM * (hd + 1), :] = r.astype(BF16)
    ones = jnp.ones((_ONES_ROWS, KB), BF16)
    vdim = 2 * B_HEAD_DIM
    for t in range(tm // KB):
        cols = slice(t * KB, (t + 1) * KB)
        for hd in range(B_HEADS):
            bvt_ref[0, t, _BV_ROWS * hd:_BV_ROWS * hd + vdim, :] = (
                rt_scr[_T_BV + vdim * hd:_T_BV + vdim * (hd + 1), cols].astype(BF16))
            bvt_ref[0, t, _BV_ROWS * hd + vdim:_BV_ROWS * (hd + 1), :] = ones
        avt_ref[0, t, 0:A_HEAD_DIM, :] = rt_scr[_T_AV:_T_AV + A_HEAD_DIM, cols].astype(BF16)
        avt_ref[0, t, A_HEAD_DIM:_AV_ROWS, :] = ones
    iwt_ref[0] = rt_scr[_T_IW:_T_IW + IDX_HEADS, :]


def _in_proj(x2, ln_g, ln_b, wn, wt, tabs, bsz, seq, tm):
    d = x2.shape[1]
    spb = seq // tm
    tpk = tm // KB
    cq, sq, ci, si, nk, mk = tabs
    tok = lambda i: (i, 0)
    tokt = lambda i: (i // spb, 0, i % spb)
    const2 = lambda i: (0, 0)
    return pl.pallas_call(
        _in_proj_kernel,
        grid=(bsz * spb,),
        in_specs=[pl.BlockSpec((tm, d), tok),
                  pl.BlockSpec((1, d), const2), pl.BlockSpec((1, d), const2),
                  pl.BlockSpec(wn.shape, const2), pl.BlockSpec(wt.shape, const2),
                  pl.BlockSpec((8, tm), lambda i: (0, i % spb)), pl.BlockSpec((8, tm), lambda i: (0, i % spb)),
                  pl.BlockSpec((8, tm), lambda i: (0, i % spb)), pl.BlockSpec((8, tm), lambda i: (0, i % spb)),
                  pl.BlockSpec((3, tm, LANES), lambda i: (0, i % spb, 0)),
                  pl.BlockSpec((5, tm, LANES), lambda i: (0, i % spb, 0))],
        out_specs=[pl.BlockSpec((1, A_WIDTH, tm), tokt),
                   pl.BlockSpec((1, 2 * B_WIDTH, tm), tokt),
                   pl.BlockSpec((1, IDX_HEADS * IDX_DIM, tm), tokt),
                   pl.BlockSpec((1, IDX_HEADS, tm), tokt),
                   pl.BlockSpec((1, tpk, _AV_ROWS, KB), lambda i: (i // spb, i % spb, 0, 0)),
                   pl.BlockSpec((1, tpk, B_HEADS * _BV_ROWS, KB), lambda i: (i // spb, i % spb, 0, 0)),
                   pl.BlockSpec((tm, B_WIDTH), tok),
                   pl.BlockSpec((tm, A_HEAD_DIM), tok),
                   pl.BlockSpec((tm, IDX_DIM), tok)],
        out_shape=[jax.ShapeDtypeStruct((bsz, A_WIDTH, seq), BF16),
                   jax.ShapeDtypeStruct((bsz, 2 * B_WIDTH, seq), BF16),
                   jax.ShapeDtypeStruct((bsz, IDX_HEADS * IDX_DIM, seq), BF16),
                   jax.ShapeDtypeStruct((bsz, IDX_HEADS, seq), F32),
                   jax.ShapeDtypeStruct((bsz, seq // KB, _AV_ROWS, KB), BF16),
                   jax.ShapeDtypeStruct((bsz, seq // KB, B_HEADS * _BV_ROWS, KB), BF16),
                   jax.ShapeDtypeStruct((bsz * seq, B_WIDTH), BF16),
                   jax.ShapeDtypeStruct((bsz * seq, A_HEAD_DIM), BF16),
                   jax.ShapeDtypeStruct((bsz * seq, IDX_DIM), BF16)],
        scratch_shapes=[pltpu.VMEM((_T_ROWS, tm), F32)],
        compiler_params=pltpu.CompilerParams(vmem_limit_bytes=VMEM_LIMIT),
        name="in_proj",
    )(x2, ln_g, ln_b, wn, wt, cq, sq, ci, si, nk, mk)


def _query_limit(j):
    lane = lax.broadcasted_iota(I32, (1, QB), 1)
    return j * QB + (lane // CHUNK + 1) * CHUNK


def _key_index(off):
    return off + lax.broadcasted_iota(I32, (KB, QB), 0)


def _online_softmax(last, logits, values, vrows, s_bufs, mx_bufs, m_scr, acc_scr):
    def stage_logits(kb, buf, diag):
        for mp, s in enumerate(logits(kb, diag)):
            sb = s.astype(BF16)
            s_bufs[buf][mp] = sb
            mx_bufs[buf][mp, 0:1, :] = jnp.max(sb, axis=0, keepdims=True).astype(F32)

    def stage_values(kb, buf):
        for mp, vt in enumerate(values(kb)):
            rows = slice(vrows * mp, vrows * (mp + 1))
            m_old = m_scr[mp, 0:1, :]
            m_new = jnp.maximum(m_old, mx_bufs[buf][mp, 0:1, :])
            alpha = jnp.exp2(m_old - m_new)
            p = jnp.exp2(s_bufs[buf][mp] - m_new.astype(BF16))
            acc_scr[rows, :] = alpha * acc_scr[rows, :] + _dot(vt, p)
            m_scr[mp, 0:1, :] = m_new

    def pair(t, carry):
        kb = 2 * t
        stage_logits(kb + 1, 1, False)
        stage_values(kb, 0)
        stage_logits(kb + 2, 0, False)
        stage_values(kb + 1, 1)
        return carry

    pl.when(last == 0)(lambda: stage_logits(0, 0, True))
    pl.when(last > 0)(lambda: stage_logits(0, 0, False))
    trips = jnp.maximum(last - 1, 0) // 2
    lax.fori_loop(0, trips, pair, 0)
    base = 2 * trips
    rem = last - base

    @pl.when(rem == 0)
    def _():
        stage_values(last, 0)

    @pl.when(rem == 1)
    def _():
        stage_logits(last, 1, True)
        stage_values(base, 0)
        stage_values(last, 1)

    @pl.when(rem == 2)
    def _():
        stage_logits(base + 1, 1, False)
        stage_values(base, 0)
        stage_logits(last, 0, True)
        stage_values(base + 1, 1)
        stage_values(last, 0)


def _mixer_a_kernel(n_sel, nqb, aqt_ref, iqt_ref, iwt_ref, ak_ref, ik_ref, avt_ref, o_ref,
                    keys_scr, hi_scr, lo_scr, thr_scr, s0_scr, s1_scr, mx0_scr, mx1_scr, m_scr, acc_scr):
    s_bufs, mx_bufs = (s0_scr, s1_scr), (mx0_scr, mx1_scr)
    j = pl.program_id(1)
    nkb = j + 1
    limit = _query_limit(j)
    iw = iwt_ref[0]

    def score_block(kb, carry):
        off = pl.multiple_of(kb * KB, KB)
        ikb = ik_ref[0, pl.ds(off, KB), :]
        sc = jnp.zeros((KB, QB), F32)
        for hd in range(IDX_HEADS):
            z = _dot(ikb, iqt_ref[0, IDX_DIM * hd:IDX_DIM * (hd + 1), :])
            sc = sc + iw[hd:hd + 1, :] * jnp.maximum(z, 0.0)
        bits = pltpu.bitcast(sc, I32)
        key = bits ^ ((bits >> 31) & 0x7FFFFFFF)
        key = jnp.where(_key_index(off) < limit, key, INT_MIN)
        keys_scr[pl.ds(off, KB), :] = key
        hi_scr[pl.ds(off, KB), :] = (key >> 16).astype(I16)
        lo_scr[pl.ds(off, KB), :] = ((key & 0xFFFF) - HALF16).astype(I16)
        return carry

    lax.fori_loop(0, nkb, score_block, 0)

    def select_threshold(nblk):
        def count_ge(src_scr, cand):
            cand16 = cand.astype(I16)
            part = jnp.zeros((PACK16, QB), I16)
            for kb in range(nblk):
                hit = jnp.where(src_scr[kb * KB:(kb + 1) * KB, :] >= cand16, jnp.int16(1), jnp.int16(0))
                for r in range(KB // PACK16):
                    part = part + hit[PACK16 * r:PACK16 * (r + 1)]
            return jnp.sum(part.astype(I32), axis=0, keepdims=True)

        def radix16(src_scr, need, cnt0):
            def step(it, carry):
                thr, cnt = carry
                cand = thr + lax.shift_left(jnp.int32(1), 15 - it)
                c = count_ge(src_scr, cand)
                take = c >= need
                return jnp.where(take, cand, thr), jnp.where(take, c, cnt)
            return lax.fori_loop(0, 16, step, (jnp.full((1, QB), -HALF16, I32), cnt0))

        t_hi, cnt_hi = radix16(hi_scr, n_sel, jnp.full((1, QB), nblk * KB, I32))
        above = jnp.where(t_hi == HALF16 - 1, 0, count_ge(hi_scr, t_hi + 1))
        t_hi16 = t_hi.astype(I16)
        for kb in range(nblk):
            rows = slice(kb * KB, (kb + 1) * KB)
            lo_scr[rows, :] = jnp.where(hi_scr[rows, :] == t_hi16, lo_scr[rows, :], jnp.int16(-HALF16))
        t_lo, cnt_lo = radix16(lo_scr, n_sel - above, cnt_hi - above)
        thr_scr[0:1, :] = t_hi * (2 * HALF16) + (t_lo + HALF16)
        thr_scr[1:2, :] = above + cnt_lo

    for k in range(nqb):
        pl.when(j == k)(functools.partial(select_threshold, k + 1))
    thr = thr_scr[0:1, :]
    cnt = thr_scr[1:2, :]

    def count(pred_fn):
        def body(kb, c):
            off = pl.multiple_of(kb * KB, KB)
            hit = pred_fn(keys_scr[pl.ds(off, KB), :])
            return c + jnp.sum(jnp.where(hit, 1.0, 0.0), axis=0, keepdims=True)
        return lax.fori_loop(0, nkb, body, jnp.zeros((1, QB), F32))

    tied = (cnt > n_sel) & (thr > INT_MIN)

    @pl.when(jnp.max(jnp.where(tied, 1.0, 0.0)) > 0.0)
    def _():
        need = n_sel - count(lambda blk: blk > thr)
        thr_tied = jnp.where(tied, thr, INT_MIN)
        tri = (lax.broadcasted_iota(I32, (KB, KB), 0) >= lax.broadcasted_iota(I32, (KB, KB), 1)).astype(BF16)

        def body(kb, seen):
            off = pl.multiple_of(kb * KB, KB)
            blk = keys_scr[pl.ds(off, KB), :]
            eq = blk == thr_tied
            eqb = jnp.where(eq, 1.0, 0.0).astype(BF16)
            rank = _dot(tri, eqb) + seen
            keys_scr[pl.ds(off, KB), :] = jnp.where(eq & (rank > need), INT_MIN, blk)
            return seen + jnp.sum(eqb.astype(F32), axis=0, keepdims=True)

        lax.fori_loop(0, nkb, body, jnp.zeros((1, QB), F32))

    thr = jnp.maximum(thr, INT_MIN + 1)

    m_scr[...] = jnp.full(m_scr.shape, NEG, F32)
    acc_scr[...] = jnp.zeros(acc_scr.shape, F32)

    def logits(kb, diag):
        del diag
        off = pl.multiple_of(kb * KB, KB)
        sel = keys_scr[pl.ds(off, KB), :] >= thr
        akb = ak_ref[0, pl.ds(off, KB), :]
        for hd in range(A_HEADS):
            s = _dot(akb, aqt_ref[0, A_HEAD_DIM * hd:A_HEAD_DIM * (hd + 1), :])
            yield jnp.where(sel, s, NEG)

    def values(kb):
        avb = avt_ref[0, kb]
        return [avb] * A_HEADS

    _online_softmax(j, logits, values, _AV_ROWS, s_bufs, mx_bufs, m_scr, acc_scr)

    outs = []
    for hd in range(A_HEADS):
        base = _AV_ROWS * hd
        outs.append(acc_scr[base:base + A_HEAD_DIM, :] / acc_scr[base + A_HEAD_DIM:base + A_HEAD_DIM + 1, :])
    o_ref[0] = jnp.concatenate(outs, axis=0).T.astype(BF16)


def _mixer_a(aqt, iqt, iwt, ak, ik, avt, bsz, seq):
    nqb = seq // QB
    qblk = lambda b, j: (b, 0, j)
    full3 = lambda b, j: (b, 0, 0)
    n_sel = min(TOPK_MAX, seq // 4)
    return pl.pallas_call(
        functools.partial(_mixer_a_kernel, n_sel, nqb),
        grid=(bsz, nqb),
        in_specs=[pl.BlockSpec((1, A_WIDTH, QB), qblk),
                  pl.BlockSpec((1, IDX_HEADS * IDX_DIM, QB), qblk),
                  pl.BlockSpec((1, IDX_HEADS, QB), qblk),
                  pl.BlockSpec((1, seq, A_HEAD_DIM), full3),
                  pl.BlockSpec((1, seq, IDX_DIM), full3),
                  pl.BlockSpec((1, seq // KB, _AV_ROWS, KB), lambda b, j: (b, 0, 0, 0))],
        out_specs=pl.BlockSpec((1, QB, A_WIDTH), lambda b, j: (b, j, 0)),
        out_shape=jax.ShapeDtypeStruct((bsz, seq, A_WIDTH), BF16),
        scratch_shapes=[pltpu.VMEM((seq, QB), I32),
                        pltpu.VMEM((seq, QB), I16),
                        pltpu.VMEM((seq, QB), I16),
                        pltpu.VMEM((8, QB), I32),
                        pltpu.VMEM((A_HEADS, KB, QB), BF16),
                        pltpu.VMEM((A_HEADS, KB, QB), BF16),
                        pltpu.VMEM((A_HEADS, 8, QB), F32),
                        pltpu.VMEM((A_HEADS, 8, QB), F32),
                        pltpu.VMEM((A_HEADS, 8, QB), F32),
                        pltpu.VMEM((A_HEADS * _AV_ROWS, QB), F32)],
        compiler_params=pltpu.CompilerParams(vmem_limit_bytes=VMEM_LIMIT),
        name="mixer_a",
    )(aqt, iqt, iwt, ak, ik, avt)


def _mixer_b_kernel(lam_init, bqt_ref, bk_ref, bvt_ref, dl_ref, g_ref, o_ref,
                    s0_scr, s1_scr, mx0_scr, mx1_scr, m_scr, acc_scr):
    s_bufs, mx_bufs = (s0_scr, s1_scr), (mx0_scr, mx1_scr)
    j = pl.program_id(1)
    limit = _query_limit(j)
    vdim = 2 * B_HEAD_DIM
    nmaps = 2 * B_HEADS

    m_scr[...] = jnp.full(m_scr.shape, NEG, F32)
    acc_scr[...] = jnp.zeros(acc_scr.shape, F32)

    def logits(kb, diag):
        off = pl.multiple_of(kb * KB, KB)
        vis = (_key_index(off) < limit) if diag else None
        for mp in range(nmaps):
            kpair = bk_ref[0, pl.ds(off, KB), vdim * (mp // 2):vdim * (mp // 2 + 1)]
            s = _dot(kpair, bqt_ref[0, vdim * mp:vdim * (mp + 1), :])
            yield jnp.where(vis, s, NEG) if diag else s

    def values(kb):
        return [bvt_ref[0, kb, _BV_ROWS * (mp // 2):_BV_ROWS * (mp // 2 + 1), :] for mp in range(nmaps)]

    _online_softmax(j, logits, values, _BV_ROWS, s_bufs, mx_bufs, m_scr, acc_scr)

    dl = dl_ref[...]
    lam = (jnp.exp(jnp.sum(dl[0:1] * dl[1:2], axis=1, keepdims=True))
           - jnp.exp(jnp.sum(dl[2:3] * dl[3:4], axis=1, keepdims=True)) + lam_init)
    gain = g_ref[...] * (1.0 - lam_init)
    outs = []
    for hd in range(B_HEADS):
        b1, b2 = _BV_ROWS * 2 * hd, _BV_ROWS * (2 * hd + 1)
        o1 = acc_scr[b1:b1 + vdim, :] / acc_scr[b1 + vdim:b1 + vdim + 1, :]
        o2 = acc_scr[b2:b2 + vdim, :] / acc_scr[b2 + vdim:b2 + vdim + 1, :]
        o = o1 - lam * o2
        ms = jnp.mean(o * o, axis=0, keepdims=True)
        outs.append(o * lax.rsqrt(ms + LN_EPS) * gain)
    o_ref[0] = jnp.concatenate(outs, axis=0).T.astype(BF16)


def _mixer_b(bqt, bk, bvt, dl, gcol, lam_init, bsz, seq):
    nqb = seq // QB
    return pl.pallas_call(
        functools.partial(_mixer_b_kernel, lam_init),
        grid=(bsz, nqb),
        in_specs=[pl.BlockSpec((1, 2 * B_WIDTH, QB), lambda b, j: (b, 0, j)),
                  pl.BlockSpec((1, seq, B_WIDTH), lambda b, j: (b, 0, 0)),
                  pl.BlockSpec((1, seq // KB, B_HEADS * _BV_ROWS, KB), lambda b, j: (b, 0, 0, 0)),
                  pl.BlockSpec((4, B_HEAD_DIM), lambda b, j: (0, 0)),
                  pl.BlockSpec((2 * B_HEAD_DIM, 1), lambda b, j: (0, 0))],
        out_specs=pl.BlockSpec((1, QB, B_WIDTH), lambda b, j: (b, j, 0)),
        out_shape=jax.ShapeDtypeStruct((bsz, seq, B_WIDTH), BF16),
        scratch_shapes=[pltpu.VMEM((2 * B_HEADS, KB, QB), BF16),
                        pltpu.VMEM((2 * B_HEADS, KB, QB), BF16),
                        pltpu.VMEM((2 * B_HEADS, 8, QB), F32),
                        pltpu.VMEM((2 * B_HEADS, 8, QB), F32),
                        pltpu.VMEM((2 * B_HEADS, 8, QB), F32),
                        pltpu.VMEM((2 * B_HEADS * _BV_ROWS, QB), F32)],
        compiler_params=pltpu.CompilerParams(vmem_limit_bytes=VMEM_LIMIT),
        name="mixer_b",
    )(bqt, bk, bvt, dl, gcol)


_TAIL_ROWS = 256


def _silu(x):
    return x * jax.nn.sigmoid(x)


def _tail_kernel(x_ref, oa_ref, ob_ref, mkt_ref, mv_ref, lig_ref, lib_ref, wcq_ref, wg_ref, wm_ref,
                 wpa_ref, wpb_ref, wpc_ref, wo_ref, lg_ref, lb_ref, out_ref):
    tm = x_ref.shape[0]
    groups = [slice(r, r + _TAIL_ROWS) for r in range(0, tm, _TAIL_ROWS)]
    each = lambda fn, *lists: [fn(*args) for args in zip(*lists)]

    h = [_layer_norm(x_ref[g, :], lig_ref[...], lib_ref[...]) for g in groups]
    hb = each(lambda v: v.astype(BF16), h)
    cq = each(lambda v: _dot(v, wcq_ref[...]).astype(BF16), hb)
    gate_cols = lambda n: slice(A_WIDTH * n, A_WIDTH * (n + 1))
    merge_cols = lambda n: slice(D_MODEL * n, D_MODEL * (n + 1))
    silu_gate = lambda n: each(lambda v: _silu(_dot(v, wg_ref[:, gate_cols(n)])), hb)
    merge_gate = lambda n: each(lambda v: jax.nn.sigmoid(_dot(v, wm_ref[:, merge_cols(n)])), hb)
    u_a = each(lambda g, sg: (oa_ref[0, g, :].astype(F32) * sg).astype(BF16), groups, silu_gate(0))
    u_b = each(lambda g, sg: (ob_ref[0, g, :].astype(F32) * sg).astype(BF16), groups, silu_gate(1))
    sg_c = silu_gate(2)

    heads = [slice(C_HEAD_DIM * hd, C_HEAD_DIM * (hd + 1)) for hd in range(C_HEADS)]

    def mem_probs(q):
        out = []
        for cols in heads:
            s = _dot(q[:, cols], mkt_ref[0, cols, :]) * (C_HEAD_DIM ** -0.5)
            p = jnp.exp(s - jnp.max(s, axis=-1, keepdims=True))
            out.append((p / jnp.sum(p, axis=-1, keepdims=True)).astype(BF16))
        return out

    probs = each(mem_probs, cq)
    mg_a = merge_gate(0)
    o_c = each(lambda ps: jnp.concatenate([_dot(p, mv_ref[0, :, cols]) for p, cols in zip(ps, heads)], axis=1),
               probs)
    merged = each(lambda u, mg: mg * _dot(u, wpa_ref[...]), u_a, mg_a)
    mg_b = merge_gate(1)
    merged = each(lambda m, u, mg: m + mg * _dot(u, wpb_ref[...]), merged, u_b, mg_b)
    mg_c = merge_gate(2)
    merged = each(lambda m, o, sg, mg: m + mg * _dot((o * sg).astype(BF16), wpc_ref[...]), merged, o_c, sg_c, mg_c)
    z = each(lambda hh, m: DEEPNORM_ALPHA * hh + _dot(m.astype(BF16), wo_ref[...]), h, merged)
    for g, zz in zip(groups, z):
        out_ref[g, :] = _layer_norm(zz, lg_ref[...], lb_ref[...])


def _tail(x2, o_a, o_b, mkt, mv, lig, lib, wcq, wg, wm, wpa, wpb, wpc, wo, lg, lb, bsz, seq, tm):
    d = x2.shape[1]
    spb = seq // tm
    mlen = mv.shape[1]
    tok = lambda i: (i, 0)
    tok3 = lambda i: (i // spb, i % spb, 0)
    bat3 = lambda i: (i // spb, 0, 0)
    const2 = lambda i: (0, 0)
    wspec = lambda w: pl.BlockSpec(w.shape, const2, pipeline_mode=pl.Buffered(1))
    vec = pl.BlockSpec((1, d), const2)
    return pl.pallas_call(
        _tail_kernel,
        grid=(bsz * spb,),
        in_specs=[pl.BlockSpec((tm, d), tok),
                  pl.BlockSpec((1, tm, A_WIDTH), tok3),
                  pl.BlockSpec((1, tm, B_WIDTH), tok3),
                  pl.BlockSpec((1, C_WIDTH, mlen), bat3),
                  pl.BlockSpec((1, mlen, C_WIDTH), bat3),
                  vec, vec, wspec(wcq), wspec(wg), wspec(wm), wspec(wpa), wspec(wpb), wspec(wpc), wspec(wo),
                  vec, vec],
        out_specs=pl.BlockSpec((tm, d), tok),
        out_shape=jax.ShapeDtypeStruct((bsz * seq, d), F32),
        compiler_params=pltpu.CompilerParams(vmem_limit_bytes=VMEM_LIMIT),
        name="tail",
    )(x2, o_a, o_b, mkt, mv, lig, lib, wcq, wg, wm, wpa, wpb, wpc, wo, lg, lb)


def _rope_tables(seq):
    pos = jnp.arange(seq, dtype=F32)

    def cos_sin(dim):
        r = dim // ROPE_FRACTION
        half = r // 2
        inv = jnp.power(ROPE_THETA, -jnp.arange(half, dtype=F32) * (2.0 / r))
        ang = pos[:, None] * inv[None, :]
        return jnp.cos(ang), jnp.sin(ang), half

    c64, s64, h64 = cos_sin(A_HEAD_DIM)
    c32, s32, h32 = cos_sin(IDX_DIM)
    cq, sq = c64.T, s64.T
    ci = jnp.concatenate([c32.T, c32.T], axis=0)
    si = jnp.concatenate([-s32.T, s32.T], axis=0)

    def natural(cos, sin, half, dim, width):
        zero = jnp.zeros((seq, dim - 2 * half), F32)
        c = jnp.concatenate([cos, cos, jnp.ones_like(zero)], axis=1)
        a = jnp.concatenate([jnp.zeros_like(sin), sin, zero], axis=1)
        b = jnp.concatenate([-sin, jnp.zeros_like(sin), zero], axis=1)
        rep = width // dim
        return [jnp.tile(t, (1, rep)) for t in (c, a, b)]

    nk = jnp.stack(natural(c64, s64, h64, A_HEAD_DIM, LANES))
    ca, aa, ba = natural(c64, s64, h64, A_HEAD_DIM, A_HEAD_DIM)
    cb, ab, bb = natural(c32, s32, h32, IDX_DIM, IDX_DIM)
    pad = LANES - A_HEAD_DIM - IDX_DIM
    z64, z32, zp = jnp.zeros((seq, A_HEAD_DIM), F32), jnp.zeros((seq, IDX_DIM), F32), jnp.zeros((seq, pad), F32)
    mk = jnp.stack([jnp.concatenate([ca, cb, zp], axis=1),
                    jnp.concatenate([aa, z32, zp], axis=1),
                    jnp.concatenate([ba, z32, zp], axis=1),
                    jnp.concatenate([z64, ab, zp], axis=1),
                    jnp.concatenate([z64, bb, zp], axis=1)])
    return cq, sq, ci, si, nk, mk


def kernel(x, mem, ln_in_g, ln_in_b, w_in, w_mem_kv, diff_lambda, diff_norm_g,
           w_proj_a, w_proj_b, w_proj_c, w_out, ln_g, ln_b):
    bsz, seq, d = x.shape
    assert d == D_MODEL and seq % QB == 0 and w_in.shape[0] == DEPTH == 1
    tm_proj = 512 if seq % 512 == 0 else QB
    tm_tail = 512 if seq % 512 == 0 else _TAIL_ROWS
    lam_init = 0.8 - 0.6 * math.exp(-0.3 * 0)

    offs = np.cumsum((0,) + IN_SPLITS)
    w = [w_in[0][:, offs[i]:offs[i + 1]] for i in range(len(IN_SPLITS))]
    (w_aq, w_ak, w_av, w_ag, w_iq, w_ik, w_iw, w_bq, w_bk, w_bv, w_bg, w_cq, w_cg, w_mg) = w
    wt = jnp.concatenate([w_aq * (A_HEAD_DIM ** -0.5 * LOG2E), w_bq * (B_HEAD_DIM ** -0.5 * LOG2E), w_iq, w_bv, w_av,
                          w_iw * ((IDX_HEADS * IDX_DIM) ** -0.5), jnp.zeros((d, 16 - IDX_HEADS), F32)],
                         axis=1).T.astype(BF16)
    wn = jnp.concatenate([w_bk, w_ak, w_ik, jnp.zeros((d, LANES - A_HEAD_DIM - IDX_DIM), F32)],
                         axis=1).astype(BF16)
    wg = jnp.concatenate([w_ag, w_bg, w_cg], axis=1).astype(BF16)
    row = lambda v: v.reshape(1, -1).astype(F32)

    x2 = x.reshape(bsz * seq, d)
    tabs = _rope_tables(seq)
    mkt, mv = _mem_kv(mem, w_mem_kv[0][:, :C_WIDTH].T.astype(BF16), w_mem_kv[0][:, C_WIDTH:].astype(BF16))
    aqt, bqt, iqt, iwt, avt, bvt, bk, ak, ik = _in_proj(
        x2, row(ln_in_g), row(ln_in_b), wn, wt, tabs, bsz, seq, tm_proj)
    o_a = _mixer_a(aqt, iqt, iwt, ak.reshape(bsz, seq, A_HEAD_DIM), ik.reshape(bsz, seq, IDX_DIM), avt, bsz, seq)
    o_b = _mixer_b(bqt, bk.reshape(bsz, seq, B_WIDTH), bvt, diff_lambda[0].astype(F32),
                   diff_norm_g[0].reshape(-1, 1).astype(F32), lam_init, bsz, seq)
    out = _tail(x2, o_a, o_b, mkt, mv, row(ln_in_g), row(ln_in_b), w_cq.astype(BF16), wg, w_mg.astype(BF16),
                w_proj_a[0].astype(BF16), w_proj_b[0].astype(BF16), w_proj_c[0].astype(BF16),
                w_out[0].astype(BF16), row(ln_g[0]), row(ln_b[0]), bsz, seq, tm_tail)
    return out.reshape(bsz, seq, d)
```

```python
import functools
import math

import numpy as np
import jax
import jax.numpy as jnp
from jax import lax
from jax.experimental import pallas as pl
from jax.experimental.pallas import tpu as pltpu

D_MODEL = 1024
CHUNK = 64
ROPE_THETA = 500000.0
ROPE_FRACTION = 4
LN_EPS = 1e-5
A_HEADS = 8
A_HEAD_DIM = 64
A_WIDTH = A_HEADS * A_HEAD_DIM
IDX_HEADS = 8
IDX_DIM = 32
TOPK_MAX = 256
B_HEADS = 4
B_HEAD_DIM = 64
B_WIDTH = B_HEADS * 2 * B_HEAD_DIM
C_HEADS = 4
C_HEAD_DIM = 128
C_WIDTH = C_HEADS * C_HEAD_DIM
N_BRANCH = 3
DEPTH = 1
DEEPNORM_ALPHA = (2.0 * DEPTH) ** 0.25
IN_SPLITS = (A_WIDTH, A_HEAD_DIM, A_HEAD_DIM, A_WIDTH, IDX_HEADS * IDX_DIM, IDX_DIM, IDX_HEADS,
             B_WIDTH, B_WIDTH, B_WIDTH, B_WIDTH, C_WIDTH, C_WIDTH, N_BRANCH * D_MODEL)

LANES = 128
QB = 256
KB = 256
NEG = -1e30
INT_MIN = -2 ** 31
VMEM_LIMIT = 56 * 1024 * 1024

BF16 = jnp.bfloat16
F32 = jnp.float32
I32 = jnp.int32
SUBLANES = 8
assert KB == 32 * SUBLANES


def _nt_dot(a, b):
    return lax.dot_general(a, b, (((1,), (1,)), ((), ())), preferred_element_type=F32)


def _dot(a, b):
    return jnp.dot(a, b, preferred_element_type=F32)


def _layer_norm(x, g, b):
    mu = jnp.mean(x, axis=-1, keepdims=True)
    xc = x - mu
    var = jnp.mean(xc * xc, axis=-1, keepdims=True)
    return xc * lax.rsqrt(var + LN_EPS) * g + b


def _mem_kv_kernel(mem_ref, wkt_ref, wv_ref, mkt_ref, mv_ref):
    mb = mem_ref[0].astype(BF16)
    mkt_ref[0] = _nt_dot(wkt_ref[...], mb).astype(BF16)
    mv_ref[0] = _dot(mb, wv_ref[...]).astype(BF16)


def _mem_kv(mem, wkt, wv):
    bsz, mlen, d = mem.shape
    return pl.pallas_call(
        _mem_kv_kernel,
        grid=(bsz,),
        in_specs=[pl.BlockSpec((1, mlen, d), lambda b: (b, 0, 0)),
                  pl.BlockSpec((C_WIDTH, d), lambda b: (0, 0)),
                  pl.BlockSpec((d, C_WIDTH), lambda b: (0, 0))],
        out_specs=[pl.BlockSpec((1, C_WIDTH, mlen), lambda b: (b, 0, 0)),
                   pl.BlockSpec((1, mlen, C_WIDTH), lambda b: (b, 0, 0))],
        out_shape=[jax.ShapeDtypeStruct((bsz, C_WIDTH, mlen), BF16),
                   jax.ShapeDtypeStruct((bsz, mlen, C_WIDTH), BF16)],
        name="mem_kv",
    )(mem, wkt, wv)


_T_AQ = 0
_T_BQ = _T_AQ + A_WIDTH
_T_BV = _T_BQ + B_WIDTH
_T_AV = _T_BV + B_WIDTH
_T_ROWS = _T_AV + A_HEAD_DIM
_X_IQ = 0
_X_IK = _X_IQ + IDX_HEADS * IDX_DIM
_X_IW = _X_IK + IDX_DIM
_X_ROWS = _X_IW + 16
_IDX_K = 256
_N_COLS = B_WIDTH + LANES
_ONES_ROWS = 16
_AV_ROWS = A_HEAD_DIM + _ONES_ROWS
_BV_ROWS = 2 * B_HEAD_DIM + _ONES_ROWS
LOG2E = math.log2(math.e)


def _rope_rows64(r, cos, sin):
    x1, x2 = r[0:8], r[8:16]
    return jnp.concatenate([x1 * cos - x2 * sin, x2 * cos + x1 * sin, r[16:]], axis=0)


def _rope_rows32(r, cos2, sin2):
    top = r[0:8]
    return jnp.concatenate([top * cos2 + pltpu.roll(top, 4, 0) * sin2, r[8:]], axis=0)


def _split3(x):
    hi = x.astype(BF16)
    r = x - hi.astype(F32)
    mid = r.astype(BF16)
    lo = (r - mid.astype(F32)).astype(BF16)
    return hi, mid, lo


def _in_proj_kernel(x_ref, g_ref, b_ref, wn_ref, wt_ref, wx_ref, cq_ref, sq_ref, ci_ref, si_ref, nk_ref,
                    aqt_ref, bqt_ref, iqt_ref, iwt_ref, avt_ref, bvt_ref, bk_ref, ak_ref, ik_ref, rt_scr, xt_scr):
    tm = x_ref.shape[0]
    h = _layer_norm(x_ref[...], g_ref[...], b_ref[...])
    h_hi, h_mid, h_lo = _split3(h)
    hb = h_hi

    kn = _dot(hb, wn_ref[...])
    c64, a64, b64 = nk_ref[0], nk_ref[1], nk_ref[2]
    for g in range(_N_COLS // LANES):
        v = kn[:, g * LANES:(g + 1) * LANES]
        v = (v * c64 + pltpu.roll(v, 8, 1) * a64 + pltpu.roll(v, LANES - 8, 1) * b64).astype(BF16)
        if g < B_WIDTH // LANES:
            bk_ref[:, g * LANES:(g + 1) * LANES] = v
        else:
            ak_ref[...] = v[:, :A_HEAD_DIM]

    n = _X_ROWS
    t1 = _nt_dot(wx_ref[...], h_hi)
    t2 = _nt_dot(wx_ref[0:2 * n, :], h_mid)
    t3 = _nt_dot(wx_ref[0:n, :], h_lo)
    xt_scr[...] = (t3 + t2[n:2 * n] + t1[2 * n:3 * n]) + (t2[0:n] + t1[n:2 * n]) + t1[0:n]
    ci, si = ci_ref[...], si_ref[...]
    zpad = jnp.zeros((_IDX_K - 6 * IDX_DIM, tm), BF16)
    for hd in range(IDX_HEADS):
        qh, qm, ql = _split3(_rope_rows32(xt_scr[_X_IQ + IDX_DIM * hd:_X_IQ + IDX_DIM * (hd + 1), :], ci, si))
        iqt_ref[0, _IDX_K * hd:_IDX_K * (hd + 1), :] = jnp.concatenate([qh, qm, ql, qh, qm, qh, zpad], axis=0)
    kh, km, kl = [t.astype(F32) for t in _split3(_rope_rows32(xt_scr[_X_IK:_X_IK + IDX_DIM, :], ci, si))]
    k6 = jnp.concatenate([kh, kh, kh, km, km, kl, zpad.astype(F32)], axis=0)
    ik_ref[...] = k6.T.astype(BF16)
    iwt_ref[0] = xt_scr[_X_IW:_X_IW + IDX_HEADS, :]

    rt_scr[...] = _nt_dot(wt_ref[...], hb)
    cos, sin = cq_ref[...], sq_ref[...]
    for hd in range(A_HEADS):
        r = rt_scr[_T_AQ + 64 * hd:_T_AQ + 64 * (hd + 1), :]
        aqt_ref[0, 64 * hd:64 * (hd + 1), :] = _rope_rows64(r, cos, sin).astype(BF16)
    zeros = jnp.zeros((B_HEAD_DIM, tm), BF16)
    for mp in range(2 * B_HEADS):
        r = _rope_rows64(rt_scr[_T_BQ + 64 * mp:_T_BQ + 64 * (mp + 1), :], cos, sin).astype(BF16)
        bqt_ref[0, 128 * mp:128 * (mp + 1), :] = jnp.concatenate([r, zeros] if mp % 2 == 0 else [zeros, r], axis=0)
    ones = jnp.ones((_ONES_ROWS, KB), BF16)
    vdim = 2 * B_HEAD_DIM
    for t in range(tm // KB):
        cols = slice(t * KB, (t + 1) * KB)
        for hd in range(B_HEADS):
            bvt_ref[0, t, _BV_ROWS * hd:_BV_ROWS * hd + vdim, :] = (
                rt_scr[_T_BV + vdim * hd:_T_BV + vdim * (hd + 1), cols].astype(BF16))
            bvt_ref[0, t, _BV_ROWS * hd + vdim:_BV_ROWS * (hd + 1), :] = ones
        avt_ref[0, t, 0:A_HEAD_DIM, :] = rt_scr[_T_AV:_T_AV + A_HEAD_DIM, cols].astype(BF16)
        avt_ref[0, t, A_HEAD_DIM:_AV_ROWS, :] = ones


def _in_proj(x2, ln_g, ln_b, wn, wt, wx, tabs, bsz, seq, tm):
    d = x2.shape[1]
    spb = seq // tm
    tpk = tm // KB
    cq, sq, ci, si, nk = tabs
    tok = lambda i: (i, 0)
    tokt = lambda i: (i // spb, 0, i % spb)
    const2 = lambda i: (0, 0)
    return pl.pallas_call(
        _in_proj_kernel,
        grid=(bsz * spb,),
        in_specs=[pl.BlockSpec((tm, d), tok),
                  pl.BlockSpec((1, d), const2), pl.BlockSpec((1, d), const2),
                  pl.BlockSpec(wn.shape, const2), pl.BlockSpec(wt.shape, const2), pl.BlockSpec(wx.shape, const2),
                  pl.BlockSpec((8, tm), lambda i: (0, i % spb)), pl.BlockSpec((8, tm), lambda i: (0, i % spb)),
                  pl.BlockSpec((8, tm), lambda i: (0, i % spb)), pl.BlockSpec((8, tm), lambda i: (0, i % spb)),
                  pl.BlockSpec((3, tm, LANES), lambda i: (0, i % spb, 0))],
        out_specs=[pl.BlockSpec((1, A_WIDTH, tm), tokt),
                   pl.BlockSpec((1, 2 * B_WIDTH, tm), tokt),
                   pl.BlockSpec((1, IDX_HEADS * _IDX_K, tm), tokt),
                   pl.BlockSpec((1, IDX_HEADS, tm), tokt),
                   pl.BlockSpec((1, tpk, _AV_ROWS, KB), lambda i: (i // spb, i % spb, 0, 0)),
                   pl.BlockSpec((1, tpk, B_HEADS * _BV_ROWS, KB), lambda i: (i // spb, i % spb, 0, 0)),
                   pl.BlockSpec((tm, B_WIDTH), tok),
                   pl.BlockSpec((tm, A_HEAD_DIM), tok),
                   pl.BlockSpec((tm, _IDX_K), tok)],
        out_shape=[jax.ShapeDtypeStruct((bsz, A_WIDTH, seq), BF16),
                   jax.ShapeDtypeStruct((bsz, 2 * B_WIDTH, seq), BF16),
                   jax.ShapeDtypeStruct((bsz, IDX_HEADS * _IDX_K, seq), BF16),
                   jax.ShapeDtypeStruct((bsz, IDX_HEADS, seq), F32),
                   jax.ShapeDtypeStruct((bsz, seq // KB, _AV_ROWS, KB), BF16),
                   jax.ShapeDtypeStruct((bsz, seq // KB, B_HEADS * _BV_ROWS, KB), BF16),
                   jax.ShapeDtypeStruct((bsz * seq, B_WIDTH), BF16),
                   jax.ShapeDtypeStruct((bsz * seq, A_HEAD_DIM), BF16),
                   jax.ShapeDtypeStruct((bsz * seq, _IDX_K), BF16)],
        scratch_shapes=[pltpu.VMEM((_T_ROWS, tm), F32), pltpu.VMEM((_X_ROWS, tm), F32)],
        compiler_params=pltpu.CompilerParams(vmem_limit_bytes=VMEM_LIMIT),
        name="in_proj",
    )(x2, ln_g, ln_b, wn, wt, wx, cq, sq, ci, si, nk)


def _query_limit(j):
    lane = lax.broadcasted_iota(I32, (1, QB), 1)
    return j * QB + (lane // CHUNK + 1) * CHUNK


def _key_index(off):
    return off + lax.broadcasted_iota(I32, (KB, QB), 0)


def _bit_transpose32(words):
    a = list(words)
    j, m = 16, 0x0000FFFF
    while j:
        mask = m - (1 << 32) if m >= (1 << 31) else m
        for k in range(32):
            if k & j == 0:
                t = (a[k] ^ lax.shift_right_logical(a[k + j], jnp.int32(j))) & mask
                a[k] = a[k] ^ t
                a[k + j] = a[k + j] ^ lax.shift_left(t, jnp.int32(j))
        j >>= 1
        m = (m ^ (m << j)) & 0xFFFFFFFF
    return a


def _online_softmax(last, logits, values, vrows, s_bufs, mx_bufs, m_scr, acc_scr):
    def stage_logits(kb, buf, diag):
        for mp, s in enumerate(logits(kb, diag)):
            sb = s.astype(BF16)
            s_bufs[buf][mp] = sb
            mx_bufs[buf][mp, 0:1, :] = jnp.max(sb, axis=0, keepdims=True).astype(F32)

    def stage_values(kb, buf):
        for mp, vt in enumerate(values(kb)):
            rows = slice(vrows * mp, vrows * (mp + 1))
            m_old = m_scr[mp, 0:1, :]
            m_new = jnp.maximum(m_old, mx_bufs[buf][mp, 0:1, :])
            alpha = jnp.exp2(m_old - m_new)
            p = jnp.exp2(s_bufs[buf][mp] - m_new.astype(BF16))
            acc_scr[rows, :] = alpha * acc_scr[rows, :] + _dot(vt, p)
            m_scr[mp, 0:1, :] = m_new

    def pair(t, carry):
        kb = 2 * t
        stage_logits(kb + 1, 1, False)
        stage_values(kb, 0)
        stage_logits(kb + 2, 0, False)
        stage_values(kb + 1, 1)
        return carry

    pl.when(last == 0)(lambda: stage_logits(0, 0, True))
    pl.when(last > 0)(lambda: stage_logits(0, 0, False))
    trips = jnp.maximum(last - 1, 0) // 2
    lax.fori_loop(0, trips, pair, 0)
    base = 2 * trips
    rem = last - base

    @pl.when(rem == 0)
    def _():
        stage_values(last, 0)

    @pl.when(rem == 1)
    def _():
        stage_logits(last, 1, True)
        stage_values(base, 0)
        stage_values(last, 1)

    @pl.when(rem == 2)
    def _():
        stage_logits(base + 1, 1, False)
        stage_values(base, 0)
        stage_logits(last, 0, True)
        stage_values(base + 1, 1)
        stage_values(last, 0)


def _mixer_a_kernel(n_sel, nqb, aqt_ref, iqt_ref, iwt_ref, ak_ref, ik_ref, avt_ref, o_ref,
                    keys_scr, planes_scr, thr_scr, s0_scr, s1_scr, mx0_scr, mx1_scr, m_scr, acc_scr):
    s_bufs, mx_bufs = (s0_scr, s1_scr), (mx0_scr, mx1_scr)
    j = pl.program_id(1)
    nkb = j + 1
    limit = _query_limit(j)
    iw = iwt_ref[0]

    def score_block(kb, diag):
        off = pl.multiple_of(kb * KB, KB)
        ikb = ik_ref[0, pl.ds(off, KB), :]
        sc = jnp.zeros((KB, QB), F32)
        for hd in range(IDX_HEADS):
            z = _dot(ikb, iqt_ref[0, _IDX_K * hd:_IDX_K * (hd + 1), :])
            sc = sc + iw[hd:hd + 1, :] * jnp.maximum(z, 0.0)
        bits = pltpu.bitcast(sc, I32)
        key = bits ^ ((bits >> 31) & 0x7FFFFFFF)
        if diag:
            key = jnp.where(_key_index(off) < limit, key, INT_MIN)
        keys_scr[pl.ds(off, KB), :] = key
        ukey = key ^ INT_MIN
        planes = _bit_transpose32([ukey[SUBLANES * i:SUBLANES * (i + 1)] for i in range(32)])
        for p in range(32):
            planes_scr[kb, p] = planes[p]

    def score_full_block(kb, carry):
        score_block(kb, False)
        return carry

    lax.fori_loop(0, j, score_full_block, 0)
    score_block(j, True)

    def select_threshold(nblk):
        def tree_sum(parts):
            while len(parts) > 1:
                parts = [a + b for a, b in zip(parts[0::2], parts[1::2])] + parts[len(parts) & ~1:]
            return jnp.sum(parts[0], axis=0, keepdims=True)

        def step(it, carry):
            need, thr_u = carry[0], carry[1]
            alive = carry[2:]
            ones = [alive[kb] & planes_scr[kb, it] for kb in range(nblk)]
            c1 = tree_sum([lax.population_count(x) for x in ones])
            take = c1 >= need
            need = jnp.where(take, need, need - c1)
            thr_u = thr_u | jnp.where(take, lax.shift_left(jnp.int32(1), 31 - it), 0)
            alive = [jnp.where(take, x, a ^ x) for x, a in zip(ones, alive)]
            return (need, thr_u, *alive)

        init = (jnp.full((1, QB), n_sel, I32), jnp.zeros((1, QB), I32),
                *([jnp.full((SUBLANES, QB), -1, I32)] * nblk))
        out = lax.fori_loop(0, 32, step, init)
        need, thr_u = out[0], out[1]
        equal = tree_sum([lax.population_count(a) for a in out[2:]])
        thr_scr[0:1, :] = thr_u ^ INT_MIN
        thr_scr[1:2, :] = (n_sel - need) + equal

    for k in range(nqb):
        pl.when(j == k)(functools.partial(select_threshold, k + 1))
    thr = thr_scr[0:1, :]
    cnt = thr_scr[1:2, :]

    def count(pred_fn):
        def body(kb, c):
            off = pl.multiple_of(kb * KB, KB)
            hit = pred_fn(keys_scr[pl.ds(off, KB), :])
            return c + jnp.sum(jnp.where(hit, 1.0, 0.0), axis=0, keepdims=True)
        return lax.fori_loop(0, nkb, body, jnp.zeros((1, QB), F32))

    tied = (cnt > n_sel) & (thr > INT_MIN)

    @pl.when(jnp.max(jnp.where(tied, 1.0, 0.0)) > 0.0)
    def _():
        need = n_sel - count(lambda blk: blk > thr)
        thr_tied = jnp.where(tied, thr, INT_MIN)
        tri = (lax.broadcasted_iota(I32, (KB, KB), 0) >= lax.broadcasted_iota(I32, (KB, KB), 1)).astype(BF16)

        def body(kb, seen):
            off = pl.multiple_of(kb * KB, KB)
            blk = keys_scr[pl.ds(off, KB), :]
            eq = blk == thr_tied
            eqb = jnp.where(eq, 1.0, 0.0).astype(BF16)
            rank = _dot(tri, eqb) + seen
            keys_scr[pl.ds(off, KB), :] = jnp.where(eq & (rank > need), INT_MIN, blk)
            return seen + jnp.sum(eqb.astype(F32), axis=0, keepdims=True)

        lax.fori_loop(0, nkb, body, jnp.zeros((1, QB), F32))

    thr = jnp.maximum(thr, INT_MIN + 1)

    m_scr[...] = jnp.full(m_scr.shape, NEG, F32)
    acc_scr[...] = jnp.zeros(acc_scr.shape, F32)

    def logits(kb, diag):
        del diag
        off = pl.multiple_of(kb * KB, KB)
        sel = keys_scr[pl.ds(off, KB), :] >= thr
        akb = ak_ref[0, pl.ds(off, KB), :]
        for hd in range(A_HEADS):
            s = _dot(akb, aqt_ref[0, A_HEAD_DIM * hd:A_HEAD_DIM * (hd + 1), :])
            yield jnp.where(sel, s, NEG)

    def values(kb):
        avb = avt_ref[0, kb]
        return [avb] * A_HEADS

    _online_softmax(j, logits, values, _AV_ROWS, s_bufs, mx_bufs, m_scr, acc_scr)

    outs = []
    for hd in range(A_HEADS):
        base = _AV_ROWS * hd
        outs.append(acc_scr[base:base + A_HEAD_DIM, :] / acc_scr[base + A_HEAD_DIM:base + A_HEAD_DIM + 1, :])
    o_ref[0] = jnp.concatenate(outs, axis=0).T.astype(BF16)


def _mixer_a(aqt, iqt, iwt, ak, ik, avt, bsz, seq):
    nqb = seq // QB
    qblk = lambda b, j: (b, 0, j)
    full3 = lambda b, j: (b, 0, 0)
    n_sel = min(TOPK_MAX, seq // 4)
    return pl.pallas_call(
        functools.partial(_mixer_a_kernel, n_sel, nqb),
        grid=(bsz, nqb),
        in_specs=[pl.BlockSpec((1, A_WIDTH, QB), qblk),
                  pl.BlockSpec((1, IDX_HEADS * _IDX_K, QB), qblk),
                  pl.BlockSpec((1, IDX_HEADS, QB), qblk),
                  pl.BlockSpec((1, seq, A_HEAD_DIM), full3),
                  pl.BlockSpec((1, seq, _IDX_K), full3),
                  pl.BlockSpec((1, seq // KB, _AV_ROWS, KB), lambda b, j: (b, 0, 0, 0))],
        out_specs=pl.BlockSpec((1, QB, A_WIDTH), lambda b, j: (b, j, 0)),
        out_shape=jax.ShapeDtypeStruct((bsz, seq, A_WIDTH), BF16),
        scratch_shapes=[pltpu.VMEM((seq, QB), I32),
                        pltpu.VMEM((seq // KB, 32, SUBLANES, QB), I32),
                        pltpu.VMEM((8, QB), I32),
                        pltpu.VMEM((A_HEADS, KB, QB), BF16),
                        pltpu.VMEM((A_HEADS, KB, QB), BF16),
                        pltpu.VMEM((A_HEADS, 8, QB), F32),
                        pltpu.VMEM((A_HEADS, 8, QB), F32),
                        pltpu.VMEM((A_HEADS, 8, QB), F32),
                        pltpu.VMEM((A_HEADS * _AV_ROWS, QB), F32)],
        compiler_params=pltpu.CompilerParams(vmem_limit_bytes=VMEM_LIMIT),
        name="mixer_a",
    )(aqt, iqt, iwt, ak, ik, avt)


def _mixer_b_kernel(lam_init, bqt_ref, bk_ref, bvt_ref, dl_ref, g_ref, o_ref,
                    s0_scr, s1_scr, mx0_scr, mx1_scr, m_scr, acc_scr):
    s_bufs, mx_bufs = (s0_scr, s1_scr), (mx0_scr, mx1_scr)
    j = pl.program_id(1)
    limit = _query_limit(j)
    vdim = 2 * B_HEAD_DIM
    nmaps = 2 * B_HEADS

    m_scr[...] = jnp.full(m_scr.shape, NEG, F32)
    acc_scr[...] = jnp.zeros(acc_scr.shape, F32)

    def logits(kb, diag):
        off = pl.multiple_of(kb * KB, KB)
        vis = (_key_index(off) < limit) if diag else None
        for mp in range(nmaps):
            kpair = bk_ref[0, pl.ds(off, KB), vdim * (mp // 2):vdim * (mp // 2 + 1)]
            s = _dot(kpair, bqt_ref[0, vdim * mp:vdim * (mp + 1), :])
            yield jnp.where(vis, s, NEG) if diag else s

    def values(kb):
        return [bvt_ref[0, kb, _BV_ROWS * (mp // 2):_BV_ROWS * (mp // 2 + 1), :] for mp in range(nmaps)]

    _online_softmax(j, logits, values, _BV_ROWS, s_bufs, mx_bufs, m_scr, acc_scr)

    dl = dl_ref[...]
    lam = (jnp.exp(jnp.sum(dl[0:1] * dl[1:2], axis=1, keepdims=True))
           - jnp.exp(jnp.sum(dl[2:3] * dl[3:4], axis=1, keepdims=True)) + lam_init)
    gain = g_ref[...] * (1.0 - lam_init)
    outs = []
    for hd in range(B_HEADS):
        b1, b2 = _BV_ROWS * 2 * hd, _BV_ROWS * (2 * hd + 1)
        o1 = acc_scr[b1:b1 + vdim, :] / acc_scr[b1 + vdim:b1 + vdim + 1, :]
        o2 = acc_scr[b2:b2 + vdim, :] / acc_scr[b2 + vdim:b2 + vdim + 1, :]
        o = o1 - lam * o2
        ms = jnp.mean(o * o, axis=0, keepdims=True)
        outs.append(o * lax.rsqrt(ms + LN_EPS) * gain)
    o_ref[0] = jnp.concatenate(outs, axis=0).T.astype(BF16)


def _mixer_b(bqt, bk, bvt, dl, gcol, lam_init, bsz, seq):
    nqb = seq // QB
    return pl.pallas_call(
        functools.partial(_mixer_b_kernel, lam_init),
        grid=(bsz, nqb),
        in_specs=[pl.BlockSpec((1, 2 * B_WIDTH, QB), lambda b, j: (b, 0, j)),
                  pl.BlockSpec((1, seq, B_WIDTH), lambda b, j: (b, 0, 0)),
                  pl.BlockSpec((1, seq // KB, B_HEADS * _BV_ROWS, KB), lambda b, j: (b, 0, 0, 0)),
                  pl.BlockSpec((4, B_HEAD_DIM), lambda b, j: (0, 0)),
                  pl.BlockSpec((2 * B_HEAD_DIM, 1), lambda b, j: (0, 0))],
        out_specs=pl.BlockSpec((1, QB, B_WIDTH), lambda b, j: (b, j, 0)),
        out_shape=jax.ShapeDtypeStruct((bsz, seq, B_WIDTH), BF16),
        scratch_shapes=[pltpu.VMEM((2 * B_HEADS, KB, QB), BF16),
                        pltpu.VMEM((2 * B_HEADS, KB, QB), BF16),
                        pltpu.VMEM((2 * B_HEADS, 8, QB), F32),
                        pltpu.VMEM((2 * B_HEADS, 8, QB), F32),
                        pltpu.VMEM((2 * B_HEADS, 8, QB), F32),
                        pltpu.VMEM((2 * B_HEADS * _BV_ROWS, QB), F32)],
        compiler_params=pltpu.CompilerParams(vmem_limit_bytes=VMEM_LIMIT),
        name="mixer_b",
    )(bqt, bk, bvt, dl, gcol)


_TAIL_ROWS = 256


def _silu(x):
    return x * jax.nn.sigmoid(x)


def _tail_kernel(x_ref, oa_ref, ob_ref, mkt_ref, mv_ref, lig_ref, lib_ref, wcq_ref, wg_ref, wm_ref,
                 wpa_ref, wpb_ref, wpc_ref, wo_ref, lg_ref, lb_ref, out_ref):
    tm = x_ref.shape[0]
    groups = [slice(r, r + _TAIL_ROWS) for r in range(0, tm, _TAIL_ROWS)]
    each = lambda fn, *lists: [fn(*args) for args in zip(*lists)]

    h = [_layer_norm(x_ref[g, :], lig_ref[...], lib_ref[...]) for g in groups]
    hb = each(lambda v: v.astype(BF16), h)
    cq = each(lambda v: _dot(v, wcq_ref[...]).astype(BF16), hb)
    gate_cols = lambda n: slice(A_WIDTH * n, A_WIDTH * (n + 1))
    merge_cols = lambda n: slice(D_MODEL * n, D_MODEL * (n + 1))
    silu_gate = lambda n: each(lambda v: _silu(_dot(v, wg_ref[:, gate_cols(n)])), hb)
    merge_gate = lambda n: each(lambda v: jax.nn.sigmoid(_dot(v, wm_ref[:, merge_cols(n)])), hb)
    u_a = each(lambda g, sg: (oa_ref[0, g, :].astype(F32) * sg).astype(BF16), groups, silu_gate(0))
    u_b = each(lambda g, sg: (ob_ref[0, g, :].astype(F32) * sg).astype(BF16), groups, silu_gate(1))
    sg_c = silu_gate(2)

    heads = [slice(C_HEAD_DIM * hd, C_HEAD_DIM * (hd + 1)) for hd in range(C_HEADS)]

    def mem_probs(q):
        out = []
        for cols in heads:
            s = _dot(q[:, cols], mkt_ref[0, cols, :]) * (C_HEAD_DIM ** -0.5)
            p = jnp.exp(s - jnp.max(s, axis=-1, keepdims=True))
            out.append((p / jnp.sum(p, axis=-1, keepdims=True)).astype(BF16))
        return out

    probs = each(mem_probs, cq)
    mg_a = merge_gate(0)
    o_c = each(lambda ps: jnp.concatenate([_dot(p, mv_ref[0, :, cols]) for p, cols in zip(ps, heads)], axis=1),
               probs)
    merged = each(lambda u, mg: mg * _dot(u, wpa_ref[...]), u_a, mg_a)
    mg_b = merge_gate(1)
    merged = each(lambda m, u, mg: m + mg * _dot(u, wpb_ref[...]), merged, u_b, mg_b)
    mg_c = merge_gate(2)
    merged = each(lambda m, o, sg, mg: m + mg * _dot((o * sg).astype(BF16), wpc_ref[...]), merged, o_c, sg_c, mg_c)
    z = each(lambda hh, m: DEEPNORM_ALPHA * hh + _dot(m.astype(BF16), wo_ref[...]), h, merged)
    for g, zz in zip(groups, z):
        out_ref[g, :] = _layer_norm(zz, lg_ref[...], lb_ref[...])


def _tail(x2, o_a, o_b, mkt, mv, lig, lib, wcq, wg, wm, wpa, wpb, wpc, wo, lg, lb, bsz, seq, tm):
    d = x2.shape[1]
    spb = seq // tm
    mlen = mv.shape[1]
    tok = lambda i: (i, 0)
    tok3 = lambda i: (i // spb, i % spb, 0)
    bat3 = lambda i: (i // spb, 0, 0)
    const2 = lambda i: (0, 0)
    wspec = lambda w: pl.BlockSpec(w.shape, const2, pipeline_mode=pl.Buffered(1))
    vec = pl.BlockSpec((1, d), const2)
    return pl.pallas_call(
        _tail_kernel,
        grid=(bsz * spb,),
        in_specs=[pl.BlockSpec((tm, d), tok),
                  pl.BlockSpec((1, tm, A_WIDTH), tok3),
                  pl.BlockSpec((1, tm, B_WIDTH), tok3),
                  pl.BlockSpec((1, C_WIDTH, mlen), bat3),
                  pl.BlockSpec((1, mlen, C_WIDTH), bat3),
                  vec, vec, wspec(wcq), wspec(wg), wspec(wm), wspec(wpa), wspec(wpb), wspec(wpc), wspec(wo),
                  vec, vec],
        out_specs=pl.BlockSpec((tm, d), tok),
        out_shape=jax.ShapeDtypeStruct((bsz * seq, d), F32),
        compiler_params=pltpu.CompilerParams(vmem_limit_bytes=VMEM_LIMIT),
        name="tail",
    )(x2, o_a, o_b, mkt, mv, lig, lib, wcq, wg, wm, wpa, wpb, wpc, wo, lg, lb)


def _rope_tables(seq):
    pos = jnp.arange(seq, dtype=F32)

    def cos_sin(dim):
        r = dim // ROPE_FRACTION
        half = r // 2
        inv = jnp.power(ROPE_THETA, -jnp.arange(half, dtype=F32) * (2.0 / r))
        ang = pos[:, None] * inv[None, :]
        return jnp.cos(ang), jnp.sin(ang), half

    c64, s64, h64 = cos_sin(A_HEAD_DIM)
    c32, s32, h32 = cos_sin(IDX_DIM)
    cq, sq = c64.T, s64.T
    ci = jnp.concatenate([c32.T, c32.T], axis=0)
    si = jnp.concatenate([-s32.T, s32.T], axis=0)

    def natural(cos, sin, half, dim, width):
        zero = jnp.zeros((seq, dim - 2 * half), F32)
        c = jnp.concatenate([cos, cos, jnp.ones_like(zero)], axis=1)
        a = jnp.concatenate([jnp.zeros_like(sin), sin, zero], axis=1)
        b = jnp.concatenate([-sin, jnp.zeros_like(sin), zero], axis=1)
        rep = width // dim
        return [jnp.tile(t, (1, rep)) for t in (c, a, b)]

    nk = jnp.stack(natural(c64, s64, h64, A_HEAD_DIM, LANES))
    return cq, sq, ci, si, nk


def kernel(x, mem, ln_in_g, ln_in_b, w_in, w_mem_kv, diff_lambda, diff_norm_g,
           w_proj_a, w_proj_b, w_proj_c, w_out, ln_g, ln_b):
    bsz, seq, d = x.shape
    assert d == D_MODEL and seq % QB == 0 and w_in.shape[0] == DEPTH == 1
    tm_proj = 512 if seq % 512 == 0 else QB
    tm_tail = 512 if seq % 512 == 0 else _TAIL_ROWS
    lam_init = 0.8 - 0.6 * math.exp(-0.3 * 0)

    offs = np.cumsum((0,) + IN_SPLITS)
    w = [w_in[0][:, offs[i]:offs[i + 1]] for i in range(len(IN_SPLITS))]
    (w_aq, w_ak, w_av, w_ag, w_iq, w_ik, w_iw, w_bq, w_bk, w_bv, w_bg, w_cq, w_cg, w_mg) = w
    wt = jnp.concatenate([w_aq * (A_HEAD_DIM ** -0.5 * LOG2E), w_bq * (B_HEAD_DIM ** -0.5 * LOG2E), w_bv, w_av],
                         axis=1).T.astype(BF16)
    wx = jnp.concatenate([w_iq, w_ik, w_iw * ((IDX_HEADS * IDX_DIM) ** -0.5),
                          jnp.zeros((d, _X_ROWS - _X_IW - IDX_HEADS), F32)], axis=1).T
    wx_hi = wx.astype(BF16)
    wx_mid = (wx - wx_hi.astype(F32)).astype(BF16)
    wx_lo = (wx - wx_hi.astype(F32) - wx_mid.astype(F32)).astype(BF16)
    wx = jnp.concatenate([wx_hi, wx_mid, wx_lo], axis=0)
    wn = jnp.concatenate([w_bk, w_ak, jnp.zeros((d, LANES - A_HEAD_DIM), F32)],
                         axis=1).astype(BF16)
    wg = jnp.concatenate([w_ag, w_bg, w_cg], axis=1).astype(BF16)
    row = lambda v: v.reshape(1, -1).astype(F32)

    x2 = x.reshape(bsz * seq, d)
    tabs = _rope_tables(seq)
    mkt, mv = _mem_kv(mem, w_mem_kv[0][:, :C_WIDTH].T.astype(BF16), w_mem_kv[0][:, C_WIDTH:].astype(BF16))
    aqt, bqt, iqt, iwt, avt, bvt, bk, ak, ik = _in_proj(
        x2, row(ln_in_g), row(ln_in_b), wn, wt, wx, tabs, bsz, seq, tm_proj)
    o_a = _mixer_a(aqt, iqt, iwt, ak.reshape(bsz, seq, A_HEAD_DIM), ik.reshape(bsz, seq, _IDX_K), avt, bsz, seq)
    o_b = _mixer_b(bqt, bk.reshape(bsz, seq, B_WIDTH), bvt, diff_lambda[0].astype(F32),
                   diff_norm_g[0].reshape(-1, 1).astype(F32), lam_init, bsz, seq)
    out = _tail(x2, o_a, o_b, mkt, mv, row(ln_in_g), row(ln_in_b), w_cq.astype(BF16), wg, w_mg.astype(BF16),
                w_proj_a[0].astype(BF16), w_proj_b[0].astype(BF16), w_proj_c[0].astype(BF16),
                w_out[0].astype(BF16), row(ln_g[0]), row(ln_b[0]), bsz, seq, tm_tail)
    return out.reshape(bsz, seq, d)
```

```python
import functools
import math

import numpy as np
import jax
import jax.numpy as jnp
from jax import lax
from jax.experimental import pallas as pl
from jax.experimental.pallas import tpu as pltpu

D_MODEL = 1024
CHUNK = 64
ROPE_THETA = 500000.0
ROPE_FRACTION = 4
LN_EPS = 1e-5
A_HEADS = 8
A_HEAD_DIM = 64
A_WIDTH = A_HEADS * A_HEAD_DIM
IDX_HEADS = 8
IDX_DIM = 32
TOPK_MAX = 256
B_HEADS = 4
B_HEAD_DIM = 64
B_WIDTH = B_HEADS * 2 * B_HEAD_DIM
C_HEADS = 4
C_HEAD_DIM = 128
C_WIDTH = C_HEADS * C_HEAD_DIM
N_BRANCH = 3
DEPTH = 1
DEEPNORM_ALPHA = (2.0 * DEPTH) ** 0.25
IN_SPLITS = (A_WIDTH, A_HEAD_DIM, A_HEAD_DIM, A_WIDTH, IDX_HEADS * IDX_DIM, IDX_DIM, IDX_HEADS,
             B_WIDTH, B_WIDTH, B_WIDTH, B_WIDTH, C_WIDTH, C_WIDTH, N_BRANCH * D_MODEL)

LANES = 128
QB = 256
KB = 256
NEG = -1e30
INT_MIN = -2 ** 31
VMEM_LIMIT = 56 * 1024 * 1024

BF16 = jnp.bfloat16
F32 = jnp.float32
I32 = jnp.int32
SUBLANES = 8
assert KB == 32 * SUBLANES


def _nt_dot(a, b):
    return lax.dot_general(a, b, (((1,), (1,)), ((), ())), preferred_element_type=F32)


def _dot(a, b):
    return jnp.dot(a, b, preferred_element_type=F32)


def _layer_norm(x, g, b):
    mu = jnp.mean(x, axis=-1, keepdims=True)
    xc = x - mu
    var = jnp.mean(xc * xc, axis=-1, keepdims=True)
    return xc * lax.rsqrt(var + LN_EPS) * g + b


def _mem_kv_kernel(mem_ref, wkt_ref, wv_ref, mkt_ref, mv_ref):
    mb = mem_ref[0].astype(BF16)
    mkt_ref[0] = _nt_dot(wkt_ref[...], mb).astype(BF16)
    mv_ref[0] = _dot(mb, wv_ref[...]).astype(BF16)


def _mem_kv(mem, wkt, wv):
    bsz, mlen, d = mem.shape
    return pl.pallas_call(
        _mem_kv_kernel,
        grid=(bsz,),
        in_specs=[pl.BlockSpec((1, mlen, d), lambda b: (b, 0, 0)),
                  pl.BlockSpec((C_WIDTH, d), lambda b: (0, 0)),
                  pl.BlockSpec((d, C_WIDTH), lambda b: (0, 0))],
        out_specs=[pl.BlockSpec((1, C_WIDTH, mlen), lambda b: (b, 0, 0)),
                   pl.BlockSpec((1, mlen, C_WIDTH), lambda b: (b, 0, 0))],
        out_shape=[jax.ShapeDtypeStruct((bsz, C_WIDTH, mlen), BF16),
                   jax.ShapeDtypeStruct((bsz, mlen, C_WIDTH), BF16)],
        name="mem_kv",
    )(mem, wkt, wv)


_T_AQ = 0
_T_BQ = _T_AQ + A_WIDTH
_T_BV = _T_BQ + B_WIDTH
_T_AV = _T_BV + B_WIDTH
_T_ROWS = _T_AV + A_HEAD_DIM
_X_IQ = 0
_X_IK = _X_IQ + IDX_HEADS * IDX_DIM
_X_IW = _X_IK + IDX_DIM
_X_ROWS = _X_IW + 16
_IDX_K = 256
_N_COLS = B_WIDTH + LANES
_ONES_ROWS = 16
_AV_ROWS = A_HEAD_DIM + _ONES_ROWS
_BV_ROWS = 2 * B_HEAD_DIM + _ONES_ROWS
LOG2E = math.log2(math.e)


def _rope_rows64(r, cos, sin):
    x1, x2 = r[0:8], r[8:16]
    return jnp.concatenate([x1 * cos - x2 * sin, x2 * cos + x1 * sin, r[16:]], axis=0)


def _rope_rows32(r, cos2, sin2):
    top = r[0:8]
    return jnp.concatenate([top * cos2 + pltpu.roll(top, 4, 0) * sin2, r[8:]], axis=0)


def _split3(x):
    hi = x.astype(BF16)
    r = x - hi.astype(F32)
    mid = r.astype(BF16)
    lo = (r - mid.astype(F32)).astype(BF16)
    return hi, mid, lo


def _in_proj_kernel(x_ref, g_ref, b_ref, wn_ref, wt_ref, wx_ref, cq_ref, sq_ref, ci_ref, si_ref, nk_ref,
                    aqt_ref, bqt_ref, iqt_ref, iwt_ref, avt_ref, bvt_ref, bk_ref, ak_ref, ik_ref, rt_scr, xt_scr):
    tm = x_ref.shape[0]
    h = _layer_norm(x_ref[...], g_ref[...], b_ref[...])
    h_hi, h_mid, h_lo = _split3(h)
    hb = h_hi

    kn = _dot(hb, wn_ref[...])
    c64, a64, b64 = nk_ref[0], nk_ref[1], nk_ref[2]
    for g in range(_N_COLS // LANES):
        v = kn[:, g * LANES:(g + 1) * LANES]
        v = (v * c64 + pltpu.roll(v, 8, 1) * a64 + pltpu.roll(v, LANES - 8, 1) * b64).astype(BF16)
        if g < B_WIDTH // LANES:
            bk_ref[:, g * LANES:(g + 1) * LANES] = v
        else:
            ak_ref[...] = v[:, :A_HEAD_DIM]

    n = _X_ROWS
    t1 = _nt_dot(wx_ref[...], h_hi)
    t2 = _nt_dot(wx_ref[0:2 * n, :], h_mid)
    t3 = _nt_dot(wx_ref[0:n, :], h_lo)
    xt_scr[...] = (t3 + t2[n:2 * n] + t1[2 * n:3 * n]) + (t2[0:n] + t1[n:2 * n]) + t1[0:n]
    ci, si = ci_ref[...], si_ref[...]
    zpad = jnp.zeros((_IDX_K - 6 * IDX_DIM, tm), BF16)
    for hd in range(IDX_HEADS):
        qh, qm, ql = _split3(_rope_rows32(xt_scr[_X_IQ + IDX_DIM * hd:_X_IQ + IDX_DIM * (hd + 1), :], ci, si))
        iqt_ref[0, _IDX_K * hd:_IDX_K * (hd + 1), :] = jnp.concatenate([qh, qm, ql, qh, qm, qh, zpad], axis=0)
    kh, km, kl = [t.astype(F32) for t in _split3(_rope_rows32(xt_scr[_X_IK:_X_IK + IDX_DIM, :], ci, si))]
    k6 = jnp.concatenate([kh, kh, kh, km, km, kl, zpad.astype(F32)], axis=0)
    ik_ref[...] = k6.T.astype(BF16)
    iwt_ref[0] = xt_scr[_X_IW:_X_IW + IDX_HEADS, :]

    rt_scr[...] = _nt_dot(wt_ref[...], hb)
    cos, sin = cq_ref[...], sq_ref[...]
    for hd in range(A_HEADS):
        r = rt_scr[_T_AQ + 64 * hd:_T_AQ + 64 * (hd + 1), :]
        aqt_ref[0, 64 * hd:64 * (hd + 1), :] = _rope_rows64(r, cos, sin).astype(BF16)
    zeros = jnp.zeros((B_HEAD_DIM, tm), BF16)
    for mp in range(2 * B_HEADS):
        r = _rope_rows64(rt_scr[_T_BQ + 64 * mp:_T_BQ + 64 * (mp + 1), :], cos, sin).astype(BF16)
        bqt_ref[0, 128 * mp:128 * (mp + 1), :] = jnp.concatenate([r, zeros] if mp % 2 == 0 else [zeros, r], axis=0)
    ones = jnp.ones((_ONES_ROWS, KB), BF16)
    vdim = 2 * B_HEAD_DIM
    for t in range(tm // KB):
        cols = slice(t * KB, (t + 1) * KB)
        for hd in range(B_HEADS):
            bvt_ref[0, t, _BV_ROWS * hd:_BV_ROWS * hd + vdim, :] = (
                rt_scr[_T_BV + vdim * hd:_T_BV + vdim * (hd + 1), cols].astype(BF16))
            bvt_ref[0, t, _BV_ROWS * hd + vdim:_BV_ROWS * (hd + 1), :] = ones
        avt_ref[0, t, 0:A_HEAD_DIM, :] = rt_scr[_T_AV:_T_AV + A_HEAD_DIM, cols].astype(BF16)
        avt_ref[0, t, A_HEAD_DIM:_AV_ROWS, :] = ones


def _in_proj(x2, ln_g, ln_b, wn, wt, wx, tabs, bsz, seq, tm):
    d = x2.shape[1]
    spb = seq // tm
    tpk = tm // KB
    cq, sq, ci, si, nk = tabs
    tok = lambda i: (i, 0)
    tokt = lambda i: (i // spb, 0, i % spb)
    const2 = lambda i: (0, 0)
    return pl.pallas_call(
        _in_proj_kernel,
        grid=(bsz * spb,),
        in_specs=[pl.BlockSpec((tm, d), tok),
                  pl.BlockSpec((1, d), const2), pl.BlockSpec((1, d), const2),
                  pl.BlockSpec(wn.shape, const2), pl.BlockSpec(wt.shape, const2), pl.BlockSpec(wx.shape, const2),
                  pl.BlockSpec((8, tm), lambda i: (0, i % spb)), pl.BlockSpec((8, tm), lambda i: (0, i % spb)),
                  pl.BlockSpec((8, tm), lambda i: (0, i % spb)), pl.BlockSpec((8, tm), lambda i: (0, i % spb)),
                  pl.BlockSpec((3, tm, LANES), lambda i: (0, i % spb, 0))],
        out_specs=[pl.BlockSpec((1, A_WIDTH, tm), tokt),
                   pl.BlockSpec((1, 2 * B_WIDTH, tm), tokt),
                   pl.BlockSpec((1, IDX_HEADS * _IDX_K, tm), tokt),
                   pl.BlockSpec((1, IDX_HEADS, tm), tokt),
                   pl.BlockSpec((1, tpk, _AV_ROWS, KB), lambda i: (i // spb, i % spb, 0, 0)),
                   pl.BlockSpec((1, tpk, B_HEADS * _BV_ROWS, KB), lambda i: (i // spb, i % spb, 0, 0)),
                   pl.BlockSpec((tm, B_WIDTH), tok),
                   pl.BlockSpec((tm, A_HEAD_DIM), tok),
                   pl.BlockSpec((tm, _IDX_K), tok)],
        out_shape=[jax.ShapeDtypeStruct((bsz, A_WIDTH, seq), BF16),
                   jax.ShapeDtypeStruct((bsz, 2 * B_WIDTH, seq), BF16),
                   jax.ShapeDtypeStruct((bsz, IDX_HEADS * _IDX_K, seq), BF16),
                   jax.ShapeDtypeStruct((bsz, IDX_HEADS, seq), F32),
                   jax.ShapeDtypeStruct((bsz, seq // KB, _AV_ROWS, KB), BF16),
                   jax.ShapeDtypeStruct((bsz, seq // KB, B_HEADS * _BV_ROWS, KB), BF16),
                   jax.ShapeDtypeStruct((bsz * seq, B_WIDTH), BF16),
                   jax.ShapeDtypeStruct((bsz * seq, A_HEAD_DIM), BF16),
                   jax.ShapeDtypeStruct((bsz * seq, _IDX_K), BF16)],
        scratch_shapes=[pltpu.VMEM((_T_ROWS, tm), F32), pltpu.VMEM((_X_ROWS, tm), F32)],
        compiler_params=pltpu.CompilerParams(vmem_limit_bytes=VMEM_LIMIT),
        name="in_proj",
    )(x2, ln_g, ln_b, wn, wt, wx, cq, sq, ci, si, nk)


def _query_limit(j):
    lane = lax.broadcasted_iota(I32, (1, QB), 1)
    return j * QB + (lane // CHUNK + 1) * CHUNK


def _key_index(off):
    return off + lax.broadcasted_iota(I32, (KB, QB), 0)


def _bit_transpose32(words):
    a = list(words)
    j, m = 16, 0x0000FFFF
    while j:
        mask = m - (1 << 32) if m >= (1 << 31) else m
        for k in range(32):
            if k & j == 0:
                t = (a[k] ^ lax.shift_right_logical(a[k + j], jnp.int32(j))) & mask
                a[k] = a[k] ^ t
                a[k + j] = a[k + j] ^ lax.shift_left(t, jnp.int32(j))
        j >>= 1
        m = (m ^ (m << j)) & 0xFFFFFFFF
    return a


def _online_softmax(last, logits, values, vrows, s_bufs, mx_bufs, m_scr, acc_scr, side=None):
    def stage_logits(kb, buf, diag):
        for mp, s in enumerate(logits(kb, diag)):
            sb = s.astype(BF16)
            s_bufs[buf][mp] = sb
            mx_bufs[buf][mp, 0:1, :] = jnp.max(sb, axis=0, keepdims=True).astype(F32)

    def stage_values(kb, buf, is_last=False):
        if side is not None:
            side(kb, is_last)
        for mp, vt in enumerate(values(kb)):
            rows = slice(vrows * mp, vrows * (mp + 1))
            m_old = m_scr[mp, 0:1, :]
            m_new = jnp.maximum(m_old, mx_bufs[buf][mp, 0:1, :])
            alpha = jnp.exp2(m_old - m_new)
            p = jnp.exp2(s_bufs[buf][mp] - m_new.astype(BF16))
            acc_scr[rows, :] = alpha * acc_scr[rows, :] + _dot(vt, p)
            m_scr[mp, 0:1, :] = m_new

    def pair(t, carry):
        kb = 2 * t
        stage_logits(kb + 1, 1, False)
        stage_values(kb, 0)
        stage_logits(kb + 2, 0, False)
        stage_values(kb + 1, 1)
        return carry

    pl.when(last == 0)(lambda: stage_logits(0, 0, True))
    pl.when(last > 0)(lambda: stage_logits(0, 0, False))
    trips = jnp.maximum(last - 1, 0) // 2
    lax.fori_loop(0, trips, pair, 0)
    base = 2 * trips
    rem = last - base

    @pl.when(rem == 0)
    def _():
        stage_values(last, 0, True)

    @pl.when(rem == 1)
    def _():
        stage_logits(last, 1, True)
        stage_values(base, 0)
        stage_values(last, 1, True)

    @pl.when(rem == 2)
    def _():
        stage_logits(base + 1, 1, False)
        stage_values(base, 0)
        stage_logits(last, 0, True)
        stage_values(base + 1, 1)
        stage_values(last, 0, True)


def _mixers_kernel(n_sel, nqb, lam_init, aqt_ref, iqt_ref, iwt_ref, ak_ref, ik_ref, avt_ref,
                   bqt_ref, bk_ref, bvt_ref, dl_ref, g_ref, oa_ref, ob_ref,
                   keys_scr, planes_scr, thr_scr, s0_scr, s1_scr, mx0_scr, mx1_scr, m_scr, acc_scr):
    s_bufs, mx_bufs = (s0_scr, s1_scr), (mx0_scr, mx1_scr)
    j = pl.program_id(1)
    nkb = j + 1
    limit = _query_limit(j)
    iw = iwt_ref[0]

    def score_block(kb, diag):
        off = pl.multiple_of(kb * KB, KB)
        ikb = ik_ref[0, pl.ds(off, KB), :]
        sc = jnp.zeros((KB, QB), F32)
        for hd in range(IDX_HEADS):
            z = _dot(ikb, iqt_ref[0, _IDX_K * hd:_IDX_K * (hd + 1), :])
            sc = sc + iw[hd:hd + 1, :] * jnp.maximum(z, 0.0)
        bits = pltpu.bitcast(sc, I32)
        key = bits ^ ((bits >> 31) & 0x7FFFFFFF)
        if diag:
            key = jnp.where(_key_index(off) < limit, key, INT_MIN)
        keys_scr[pl.ds(off, KB), :] = key
        ukey = key ^ INT_MIN
        planes = _bit_transpose32([ukey[SUBLANES * i:SUBLANES * (i + 1)] for i in range(32)])
        for p in range(32):
            planes_scr[kb, p] = planes[p]

    vdim = 2 * B_HEAD_DIM
    nmaps = 2 * B_HEADS
    m_scr[...] = jnp.full(m_scr.shape, NEG, F32)
    acc_scr[...] = jnp.zeros(acc_scr.shape, F32)

    def b_logits(kb, diag):
        off = pl.multiple_of(kb * KB, KB)
        vis = (_key_index(off) < limit) if diag else None
        for mp in range(nmaps):
            kpair = bk_ref[0, pl.ds(off, KB), vdim * (mp // 2):vdim * (mp // 2 + 1)]
            s = _dot(kpair, bqt_ref[0, vdim * mp:vdim * (mp + 1), :])
            yield jnp.where(vis, s, NEG) if diag else s

    def b_values(kb):
        return [bvt_ref[0, kb, _BV_ROWS * (mp // 2):_BV_ROWS * (mp // 2 + 1), :] for mp in range(nmaps)]

    _online_softmax(j, b_logits, b_values, _BV_ROWS, s_bufs, mx_bufs, m_scr, acc_scr, side=score_block)

    dl = dl_ref[...]
    lam = (jnp.exp(jnp.sum(dl[0:1] * dl[1:2], axis=1, keepdims=True))
           - jnp.exp(jnp.sum(dl[2:3] * dl[3:4], axis=1, keepdims=True)) + lam_init)
    gain = g_ref[...] * (1.0 - lam_init)
    outs = []
    for hd in range(B_HEADS):
        b1, b2 = _BV_ROWS * 2 * hd, _BV_ROWS * (2 * hd + 1)
        o1 = acc_scr[b1:b1 + vdim, :] / acc_scr[b1 + vdim:b1 + vdim + 1, :]
        o2 = acc_scr[b2:b2 + vdim, :] / acc_scr[b2 + vdim:b2 + vdim + 1, :]
        o = o1 - lam * o2
        ms = jnp.mean(o * o, axis=0, keepdims=True)
        outs.append(o * lax.rsqrt(ms + LN_EPS) * gain)
    ob_ref[0] = jnp.concatenate(outs, axis=0).T.astype(BF16)


    def select_threshold(nblk):
        def tree_sum(parts):
            while len(parts) > 1:
                parts = [a + b for a, b in zip(parts[0::2], parts[1::2])] + parts[len(parts) & ~1:]
            return jnp.sum(parts[0], axis=0, keepdims=True)

        def step(it, carry):
            need, thr_u = carry[0], carry[1]
            alive = carry[2:]
            ones = [alive[kb] & planes_scr[kb, it] for kb in range(nblk)]
            c1 = tree_sum([lax.population_count(x) for x in ones])
            take = c1 >= need
            need = jnp.where(take, need, need - c1)
            thr_u = thr_u | jnp.where(take, lax.shift_left(jnp.int32(1), 31 - it), 0)
            alive = [jnp.where(take, x, a ^ x) for x, a in zip(ones, alive)]
            return (need, thr_u, *alive)

        init = (jnp.full((1, QB), n_sel, I32), jnp.zeros((1, QB), I32),
                *([jnp.full((SUBLANES, QB), -1, I32)] * nblk))
        out = lax.fori_loop(0, 32, step, init)
        need, thr_u = out[0], out[1]
        equal = tree_sum([lax.population_count(a) for a in out[2:]])
        thr_scr[0:1, :] = thr_u ^ INT_MIN
        thr_scr[1:2, :] = (n_sel - need) + equal

    for k in range(nqb):
        pl.when(j == k)(functools.partial(select_threshold, k + 1))
    thr = thr_scr[0:1, :]
    cnt = thr_scr[1:2, :]

    def count(pred_fn):
        def body(kb, c):
            off = pl.multiple_of(kb * KB, KB)
            hit = pred_fn(keys_scr[pl.ds(off, KB), :])
            return c + jnp.sum(jnp.where(hit, 1.0, 0.0), axis=0, keepdims=True)
        return lax.fori_loop(0, nkb, body, jnp.zeros((1, QB), F32))

    tied = (cnt > n_sel) & (thr > INT_MIN)

    @pl.when(jnp.max(jnp.where(tied, 1.0, 0.0)) > 0.0)
    def _():
        need = n_sel - count(lambda blk: blk > thr)
        thr_tied = jnp.where(tied, thr, INT_MIN)
        tri = (lax.broadcasted_iota(I32, (KB, KB), 0) >= lax.broadcasted_iota(I32, (KB, KB), 1)).astype(BF16)

        def body(kb, seen):
            off = pl.multiple_of(kb * KB, KB)
            blk = keys_scr[pl.ds(off, KB), :]
            eq = blk == thr_tied
            eqb = jnp.where(eq, 1.0, 0.0).astype(BF16)
            rank = _dot(tri, eqb) + seen
            keys_scr[pl.ds(off, KB), :] = jnp.where(eq & (rank > need), INT_MIN, blk)
            return seen + jnp.sum(eqb.astype(F32), axis=0, keepdims=True)

        lax.fori_loop(0, nkb, body, jnp.zeros((1, QB), F32))

    thr = jnp.maximum(thr, INT_MIN + 1)

    m_scr[...] = jnp.full(m_scr.shape, NEG, F32)
    acc_scr[...] = jnp.zeros(acc_scr.shape, F32)

    def logits(kb, diag):
        del diag
        off = pl.multiple_of(kb * KB, KB)
        sel = keys_scr[pl.ds(off, KB), :] >= thr
        akb = ak_ref[0, pl.ds(off, KB), :]
        for hd in range(A_HEADS):
            s = _dot(akb, aqt_ref[0, A_HEAD_DIM * hd:A_HEAD_DIM * (hd + 1), :])
            yield jnp.where(sel, s, NEG)

    def values(kb):
        avb = avt_ref[0, kb]
        return [avb] * A_HEADS

    _online_softmax(j, logits, values, _AV_ROWS, s_bufs, mx_bufs, m_scr, acc_scr)

    outs = []
    for hd in range(A_HEADS):
        base = _AV_ROWS * hd
        outs.append(acc_scr[base:base + A_HEAD_DIM, :] / acc_scr[base + A_HEAD_DIM:base + A_HEAD_DIM + 1, :])
    oa_ref[0] = jnp.concatenate(outs, axis=0).T.astype(BF16)


def _mixers(aqt, iqt, iwt, ak, ik, avt, bqt, bk, bvt, dl, gcol, lam_init, bsz, seq):
    nqb = seq // QB
    nmaps = 2 * B_HEADS
    assert nmaps == A_HEADS
    qblk = lambda b, j: (b, 0, j)
    full3 = lambda b, j: (b, 0, 0)
    full4 = lambda b, j: (b, 0, 0, 0)
    const2 = lambda b, j: (0, 0)
    oblk = lambda b, j: (b, j, 0)
    n_sel = min(TOPK_MAX, seq // 4)
    return pl.pallas_call(
        functools.partial(_mixers_kernel, n_sel, nqb, lam_init),
        grid=(bsz, nqb),
        in_specs=[pl.BlockSpec((1, A_WIDTH, QB), qblk),
                  pl.BlockSpec((1, IDX_HEADS * _IDX_K, QB), qblk),
                  pl.BlockSpec((1, IDX_HEADS, QB), qblk),
                  pl.BlockSpec((1, seq, A_HEAD_DIM), full3),
                  pl.BlockSpec((1, seq, _IDX_K), full3),
                  pl.BlockSpec((1, seq // KB, _AV_ROWS, KB), full4),
                  pl.BlockSpec((1, 2 * B_WIDTH, QB), qblk),
                  pl.BlockSpec((1, seq, B_WIDTH), full3),
                  pl.BlockSpec((1, seq // KB, B_HEADS * _BV_ROWS, KB), full4),
                  pl.BlockSpec((4, B_HEAD_DIM), const2),
                  pl.BlockSpec((2 * B_HEAD_DIM, 1), const2)],
        out_specs=[pl.BlockSpec((1, QB, A_WIDTH), oblk), pl.BlockSpec((1, QB, B_WIDTH), oblk)],
        out_shape=[jax.ShapeDtypeStruct((bsz, seq, A_WIDTH), BF16), jax.ShapeDtypeStruct((bsz, seq, B_WIDTH), BF16)],
        scratch_shapes=[pltpu.VMEM((seq, QB), I32),
                        pltpu.VMEM((seq // KB, 32, SUBLANES, QB), I32),
                        pltpu.VMEM((8, QB), I32),
                        pltpu.VMEM((nmaps, KB, QB), BF16),
                        pltpu.VMEM((nmaps, KB, QB), BF16),
                        pltpu.VMEM((nmaps, 8, QB), F32),
                        pltpu.VMEM((nmaps, 8, QB), F32),
                        pltpu.VMEM((nmaps, 8, QB), F32),
                        pltpu.VMEM((nmaps * _BV_ROWS, QB), F32)],
        compiler_params=pltpu.CompilerParams(vmem_limit_bytes=VMEM_LIMIT),
        name="mixers",
    )(aqt, iqt, iwt, ak, ik, avt, bqt, bk, bvt, dl, gcol)


_TAIL_ROWS = 256


def _silu(x):
    return x * jax.nn.sigmoid(x)


def _tail_kernel(x_ref, oa_ref, ob_ref, mkt_ref, mv_ref, lig_ref, lib_ref, wcq_ref, wg_ref, wm_ref,
                 wpa_ref, wpb_ref, wpc_ref, wo_ref, lg_ref, lb_ref, out_ref):
    tm = x_ref.shape[0]
    groups = [slice(r, r + _TAIL_ROWS) for r in range(0, tm, _TAIL_ROWS)]
    each = lambda fn, *lists: [fn(*args) for args in zip(*lists)]

    h = [_layer_norm(x_ref[g, :], lig_ref[...], lib_ref[...]) for g in groups]
    hb = each(lambda v: v.astype(BF16), h)
    cq = each(lambda v: _dot(v, wcq_ref[...]).astype(BF16), hb)
    gate_cols = lambda n: slice(A_WIDTH * n, A_WIDTH * (n + 1))
    merge_cols = lambda n: slice(D_MODEL * n, D_MODEL * (n + 1))
    silu_gate = lambda n: each(lambda v: _silu(_dot(v, wg_ref[:, gate_cols(n)])), hb)
    merge_gate = lambda n: each(lambda v: jax.nn.sigmoid(_dot(v, wm_ref[:, merge_cols(n)])), hb)
    u_a = each(lambda g, sg: (oa_ref[0, g, :].astype(F32) * sg).astype(BF16), groups, silu_gate(0))
    u_b = each(lambda g, sg: (ob_ref[0, g, :].astype(F32) * sg).astype(BF16), groups, silu_gate(1))
    sg_c = silu_gate(2)

    heads = [slice(C_HEAD_DIM * hd, C_HEAD_DIM * (hd + 1)) for hd in range(C_HEADS)]

    def mem_probs(q):
        out = []
        for cols in heads:
            s = _dot(q[:, cols], mkt_ref[0, cols, :]) * (C_HEAD_DIM ** -0.5)
            p = jnp.exp(s - jnp.max(s, axis=-1, keepdims=True))
            out.append((p / jnp.sum(p, axis=-1, keepdims=True)).astype(BF16))
        return out

    probs = each(mem_probs, cq)
    mg_a = merge_gate(0)
    o_c = each(lambda ps: jnp.concatenate([_dot(p, mv_ref[0, :, cols]) for p, cols in zip(ps, heads)], axis=1),
               probs)
    merged = each(lambda u, mg: mg * _dot(u, wpa_ref[...]), u_a, mg_a)
    mg_b = merge_gate(1)
    merged = each(lambda m, u, mg: m + mg * _dot(u, wpb_ref[...]), merged, u_b, mg_b)
    mg_c = merge_gate(2)
    merged = each(lambda m, o, sg, mg: m + mg * _dot((o * sg).astype(BF16), wpc_ref[...]), merged, o_c, sg_c, mg_c)
    z = each(lambda hh, m: DEEPNORM_ALPHA * hh + _dot(m.astype(BF16), wo_ref[...]), h, merged)
    for g, zz in zip(groups, z):
        out_ref[g, :] = _layer_norm(zz, lg_ref[...], lb_ref[...])


def _tail(x2, o_a, o_b, mkt, mv, lig, lib, wcq, wg, wm, wpa, wpb, wpc, wo, lg, lb, bsz, seq, tm):
    d = x2.shape[1]
    spb = seq // tm
    mlen = mv.shape[1]
    tok = lambda i: (i, 0)
    tok3 = lambda i: (i // spb, i % spb, 0)
    bat3 = lambda i: (i // spb, 0, 0)
    const2 = lambda i: (0, 0)
    wspec = lambda w: pl.BlockSpec(w.shape, const2, pipeline_mode=pl.Buffered(1))
    vec = pl.BlockSpec((1, d), const2)
    return pl.pallas_call(
        _tail_kernel,
        grid=(bsz * spb,),
        in_specs=[pl.BlockSpec((tm, d), tok),
                  pl.BlockSpec((1, tm, A_WIDTH), tok3),
                  pl.BlockSpec((1, tm, B_WIDTH), tok3),
                  pl.BlockSpec((1, C_WIDTH, mlen), bat3),
                  pl.BlockSpec((1, mlen, C_WIDTH), bat3),
                  vec, vec, wspec(wcq), wspec(wg), wspec(wm), wspec(wpa), wspec(wpb), wspec(wpc), wspec(wo),
                  vec, vec],
        out_specs=pl.BlockSpec((tm, d), tok),
        out_shape=jax.ShapeDtypeStruct((bsz * seq, d), F32),
        compiler_params=pltpu.CompilerParams(vmem_limit_bytes=VMEM_LIMIT),
        name="tail",
    )(x2, o_a, o_b, mkt, mv, lig, lib, wcq, wg, wm, wpa, wpb, wpc, wo, lg, lb)


def _rope_tables(seq):
    pos = jnp.arange(seq, dtype=F32)

    def cos_sin(dim):
        r = dim // ROPE_FRACTION
        half = r // 2
        inv = jnp.power(ROPE_THETA, -jnp.arange(half, dtype=F32) * (2.0 / r))
        ang = pos[:, None] * inv[None, :]
        return jnp.cos(ang), jnp.sin(ang), half

    c64, s64, h64 = cos_sin(A_HEAD_DIM)
    c32, s32, h32 = cos_sin(IDX_DIM)
    cq, sq = c64.T, s64.T
    ci = jnp.concatenate([c32.T, c32.T], axis=0)
    si = jnp.concatenate([-s32.T, s32.T], axis=0)

    def natural(cos, sin, half, dim, width):
        zero = jnp.zeros((seq, dim - 2 * half), F32)
        c = jnp.concatenate([cos, cos, jnp.ones_like(zero)], axis=1)
        a = jnp.concatenate([jnp.zeros_like(sin), sin, zero], axis=1)
        b = jnp.concatenate([-sin, jnp.zeros_like(sin), zero], axis=1)
        rep = width // dim
        return [jnp.tile(t, (1, rep)) for t in (c, a, b)]

    nk = jnp.stack(natural(c64, s64, h64, A_HEAD_DIM, LANES))
    return cq, sq, ci, si, nk


def kernel(x, mem, ln_in_g, ln_in_b, w_in, w_mem_kv, diff_lambda, diff_norm_g,
           w_proj_a, w_proj_b, w_proj_c, w_out, ln_g, ln_b):
    bsz, seq, d = x.shape
    assert d == D_MODEL and seq % QB == 0 and w_in.shape[0] == DEPTH == 1
    tm_proj = 512 if seq % 512 == 0 else QB
    tm_tail = 512 if seq % 512 == 0 else _TAIL_ROWS
    lam_init = 0.8 - 0.6 * math.exp(-0.3 * 0)

    offs = np.cumsum((0,) + IN_SPLITS)
    w = [w_in[0][:, offs[i]:offs[i + 1]] for i in range(len(IN_SPLITS))]
    (w_aq, w_ak, w_av, w_ag, w_iq, w_ik, w_iw, w_bq, w_bk, w_bv, w_bg, w_cq, w_cg, w_mg) = w
    wt = jnp.concatenate([w_aq * (A_HEAD_DIM ** -0.5 * LOG2E), w_bq * (B_HEAD_DIM ** -0.5 * LOG2E), w_bv, w_av],
                         axis=1).T.astype(BF16)
    wx = jnp.concatenate([w_iq, w_ik, w_iw * ((IDX_HEADS * IDX_DIM) ** -0.5),
                          jnp.zeros((d, _X_ROWS - _X_IW - IDX_HEADS), F32)], axis=1).T
    wx_hi = wx.astype(BF16)
    wx_mid = (wx - wx_hi.astype(F32)).astype(BF16)
    wx_lo = (wx - wx_hi.astype(F32) - wx_mid.astype(F32)).astype(BF16)
    wx = jnp.concatenate([wx_hi, wx_mid, wx_lo], axis=0)
    wn = jnp.concatenate([w_bk, w_ak, jnp.zeros((d, LANES - A_HEAD_DIM), F32)],
                         axis=1).astype(BF16)
    wg = jnp.concatenate([w_ag, w_bg, w_cg], axis=1).astype(BF16)
    row = lambda v: v.reshape(1, -1).astype(F32)

    x2 = x.reshape(bsz * seq, d)
    tabs = _rope_tables(seq)
    mkt, mv = _mem_kv(mem, w_mem_kv[0][:, :C_WIDTH].T.astype(BF16), w_mem_kv[0][:, C_WIDTH:].astype(BF16))
    aqt, bqt, iqt, iwt, avt, bvt, bk, ak, ik = _in_proj(
        x2, row(ln_in_g), row(ln_in_b), wn, wt, wx, tabs, bsz, seq, tm_proj)
    o_a, o_b = _mixers(aqt, iqt, iwt, ak.reshape(bsz, seq, A_HEAD_DIM), ik.reshape(bsz, seq, _IDX_K), avt,
                       bqt, bk.reshape(bsz, seq, B_WIDTH), bvt, diff_lambda[0].astype(F32),
                       diff_norm_g[0].reshape(-1, 1).astype(F32), lam_init, bsz, seq)
    out = _tail(x2, o_a, o_b, mkt, mv, row(ln_in_g), row(ln_in_b), w_cq.astype(BF16), wg, w_mg.astype(BF16),
                w_proj_a[0].astype(BF16), w_proj_b[0].astype(BF16), w_proj_c[0].astype(BF16),
                w_out[0].astype(BF16), row(ln_g[0]), row(ln_b[0]), bsz, seq, tm_tail)
    return out.reshape(bsz, seq, d)
```

```python
import functools
import math

import numpy as np
import jax
import jax.numpy as jnp
from jax import lax
from jax.experimental import pallas as pl
from jax.experimental.pallas import tpu as pltpu

D_MODEL = 1024
CHUNK = 64
ROPE_THETA = 500000.0
ROPE_FRACTION = 4
LN_EPS = 1e-5
A_HEADS = 8
A_HEAD_DIM = 64
A_WIDTH = A_HEADS * A_HEAD_DIM
IDX_HEADS = 8
IDX_DIM = 32
TOPK_MAX = 256
B_HEADS = 4
B_HEAD_DIM = 64
B_WIDTH = B_HEADS * 2 * B_HEAD_DIM
C_HEADS = 4
C_HEAD_DIM = 128
C_WIDTH = C_HEADS * C_HEAD_DIM
N_BRANCH = 3
DEPTH = 1
DEEPNORM_ALPHA = (2.0 * DEPTH) ** 0.25
IN_SPLITS = (A_WIDTH, A_HEAD_DIM, A_HEAD_DIM, A_WIDTH, IDX_HEADS * IDX_DIM, IDX_DIM, IDX_HEADS,
             B_WIDTH, B_WIDTH, B_WIDTH, B_WIDTH, C_WIDTH, C_WIDTH, N_BRANCH * D_MODEL)

LANES = 128
QB = 256
KB = 256
NEG = -1e30
INT_MIN = -2 ** 31
VMEM_LIMIT = 56 * 1024 * 1024

BF16 = jnp.bfloat16
F32 = jnp.float32
I32 = jnp.int32
SUBLANES = 8
assert KB == 32 * SUBLANES


def _nt_dot(a, b):
    return lax.dot_general(a, b, (((1,), (1,)), ((), ())), preferred_element_type=F32)


def _dot(a, b):
    return jnp.dot(a, b, preferred_element_type=F32)


def _layer_norm(x, g, b):
    mu = jnp.mean(x, axis=-1, keepdims=True)
    xc = x - mu
    var = jnp.mean(xc * xc, axis=-1, keepdims=True)
    return xc * lax.rsqrt(var + LN_EPS) * g + b


def _mem_kv_kernel(mem_ref, wkt_ref, wv_ref, mkt_ref, mv_ref):
    mb = mem_ref[0].astype(BF16)
    mkt_ref[0] = _nt_dot(wkt_ref[...], mb).astype(BF16)
    mv_ref[0] = _dot(mb, wv_ref[...]).astype(BF16)


def _mem_kv(mem, wkt, wv):
    bsz, mlen, d = mem.shape
    return pl.pallas_call(
        _mem_kv_kernel,
        grid=(bsz,),
        in_specs=[pl.BlockSpec((1, mlen, d), lambda b: (b, 0, 0)),
                  pl.BlockSpec((C_WIDTH, d), lambda b: (0, 0)),
                  pl.BlockSpec((d, C_WIDTH), lambda b: (0, 0))],
        out_specs=[pl.BlockSpec((1, C_WIDTH, mlen), lambda b: (b, 0, 0)),
                   pl.BlockSpec((1, mlen, C_WIDTH), lambda b: (b, 0, 0))],
        out_shape=[jax.ShapeDtypeStruct((bsz, C_WIDTH, mlen), BF16),
                   jax.ShapeDtypeStruct((bsz, mlen, C_WIDTH), BF16)],
        name="mem_kv",
    )(mem, wkt, wv)


_T_AQ = 0
_T_BQ = _T_AQ + A_WIDTH
_T_BV = _T_BQ + B_WIDTH
_T_AV = _T_BV + B_WIDTH
_T_ROWS = _T_AV + A_HEAD_DIM
_X_IQ = 0
_X_IK = _X_IQ + IDX_HEADS * IDX_DIM
_X_IW = _X_IK + IDX_DIM
_X_ROWS = _X_IW + 16
_IDX_K = 256
_N_COLS = B_WIDTH + LANES
_ONES_ROWS = 16
_AV_ROWS = A_HEAD_DIM + _ONES_ROWS
_BV_ROWS = 2 * B_HEAD_DIM + _ONES_ROWS
LOG2E = math.log2(math.e)


def _rope_rows64(r, cos, sin):
    x1, x2 = r[0:8], r[8:16]
    return jnp.concatenate([x1 * cos - x2 * sin, x2 * cos + x1 * sin, r[16:]], axis=0)


def _rope_rows32(r, cos2, sin2):
    top = r[0:8]
    return jnp.concatenate([top * cos2 + pltpu.roll(top, 4, 0) * sin2, r[8:]], axis=0)


def _split3(x):
    hi = x.astype(BF16)
    r = x - hi.astype(F32)
    mid = r.astype(BF16)
    lo = (r - mid.astype(F32)).astype(BF16)
    return hi, mid, lo


def _in_proj_kernel(x_ref, g_ref, b_ref, wn_ref, wt_ref, wx_ref, cq_ref, sq_ref, ci_ref, si_ref, nk_ref,
                    aqt_ref, bqt_ref, iqt_ref, iwt_ref, avt_ref, bvt_ref, bk_ref, ak_ref, ik_ref, rt_scr, xt_scr):
    tm = x_ref.shape[0]
    h = _layer_norm(x_ref[...], g_ref[...], b_ref[...])
    h_hi, h_mid, h_lo = _split3(h)
    hb = h_hi

    kn = _dot(hb, wn_ref[...])
    c64, a64, b64 = nk_ref[0], nk_ref[1], nk_ref[2]
    for g in range(_N_COLS // LANES):
        v = kn[:, g * LANES:(g + 1) * LANES]
        v = (v * c64 + pltpu.roll(v, 8, 1) * a64 + pltpu.roll(v, LANES - 8, 1) * b64).astype(BF16)
        if g < B_WIDTH // LANES:
            bk_ref[:, g * LANES:(g + 1) * LANES] = v
        else:
            ak_ref[...] = v[:, :A_HEAD_DIM]

    n = _X_ROWS
    t1 = _nt_dot(wx_ref[...], h_hi)
    t2 = _nt_dot(wx_ref[0:2 * n, :], h_mid)
    t3 = _nt_dot(wx_ref[0:n, :], h_lo)
    xt_scr[...] = (t3 + t2[n:2 * n] + t1[2 * n:3 * n]) + (t2[0:n] + t1[n:2 * n]) + t1[0:n]
    ci, si = ci_ref[...], si_ref[...]
    zpad = jnp.zeros((_IDX_K - 6 * IDX_DIM, tm), BF16)
    for hd in range(IDX_HEADS):
        qh, qm, ql = _split3(_rope_rows32(xt_scr[_X_IQ + IDX_DIM * hd:_X_IQ + IDX_DIM * (hd + 1), :], ci, si))
        iqt_ref[0, _IDX_K * hd:_IDX_K * (hd + 1), :] = jnp.concatenate([qh, qm, ql, qh, qm, qh, zpad], axis=0)
    kh, km, kl = [t.astype(F32) for t in _split3(_rope_rows32(xt_scr[_X_IK:_X_IK + IDX_DIM, :], ci, si))]
    k6 = jnp.concatenate([kh, kh, kh, km, km, kl, zpad.astype(F32)], axis=0)
    ik_ref[...] = k6.T.astype(BF16)
    iwt_ref[0] = xt_scr[_X_IW:_X_IW + IDX_HEADS, :]

    rt_scr[...] = _nt_dot(wt_ref[...], hb)
    cos, sin = cq_ref[...], sq_ref[...]
    for hd in range(A_HEADS):
        r = rt_scr[_T_AQ + 64 * hd:_T_AQ + 64 * (hd + 1), :]
        aqt_ref[0, 64 * hd:64 * (hd + 1), :] = _rope_rows64(r, cos, sin).astype(BF16)
    zeros = jnp.zeros((B_HEAD_DIM, tm), BF16)
    for mp in range(2 * B_HEADS):
        r = _rope_rows64(rt_scr[_T_BQ + 64 * mp:_T_BQ + 64 * (mp + 1), :], cos, sin).astype(BF16)
        bqt_ref[0, 128 * mp:128 * (mp + 1), :] = jnp.concatenate([r, zeros] if mp % 2 == 0 else [zeros, r], axis=0)
    ones = jnp.ones((_ONES_ROWS, KB), BF16)
    vdim = 2 * B_HEAD_DIM
    for t in range(tm // KB):
        cols = slice(t * KB, (t + 1) * KB)
        for hd in range(B_HEADS):
            bvt_ref[0, t, _BV_ROWS * hd:_BV_ROWS * hd + vdim, :] = (
                rt_scr[_T_BV + vdim * hd:_T_BV + vdim * (hd + 1), cols].astype(BF16))
            bvt_ref[0, t, _BV_ROWS * hd + vdim:_BV_ROWS * (hd + 1), :] = ones
        avt_ref[0, t, 0:A_HEAD_DIM, :] = rt_scr[_T_AV:_T_AV + A_HEAD_DIM, cols].astype(BF16)
        avt_ref[0, t, A_HEAD_DIM:_AV_ROWS, :] = ones


def _in_proj(x2, ln_g, ln_b, wn, wt, wx, tabs, bsz, seq, tm):
    d = x2.shape[1]
    spb = seq // tm
    tpk = tm // KB
    cq, sq, ci, si, nk = tabs
    tok = lambda i: (i, 0)
    tokt = lambda i: (i // spb, 0, i % spb)
    const2 = lambda i: (0, 0)
    return pl.pallas_call(
        _in_proj_kernel,
        grid=(bsz * spb,),
        in_specs=[pl.BlockSpec((tm, d), tok),
                  pl.BlockSpec((1, d), const2), pl.BlockSpec((1, d), const2),
                  pl.BlockSpec(wn.shape, const2), pl.BlockSpec(wt.shape, const2), pl.BlockSpec(wx.shape, const2),
                  pl.BlockSpec((8, tm), lambda i: (0, i % spb)), pl.BlockSpec((8, tm), lambda i: (0, i % spb)),
                  pl.BlockSpec((8, tm), lambda i: (0, i % spb)), pl.BlockSpec((8, tm), lambda i: (0, i % spb)),
                  pl.BlockSpec((3, tm, LANES), lambda i: (0, i % spb, 0))],
        out_specs=[pl.BlockSpec((1, A_WIDTH, tm), tokt),
                   pl.BlockSpec((1, 2 * B_WIDTH, tm), tokt),
                   pl.BlockSpec((1, IDX_HEADS * _IDX_K, tm), tokt),
                   pl.BlockSpec((1, IDX_HEADS, tm), tokt),
                   pl.BlockSpec((1, tpk, _AV_ROWS, KB), lambda i: (i // spb, i % spb, 0, 0)),
                   pl.BlockSpec((1, tpk, B_HEADS * _BV_ROWS, KB), lambda i: (i // spb, i % spb, 0, 0)),
                   pl.BlockSpec((tm, B_WIDTH), tok),
                   pl.BlockSpec((tm, A_HEAD_DIM), tok),
                   pl.BlockSpec((tm, _IDX_K), tok)],
        out_shape=[jax.ShapeDtypeStruct((bsz, A_WIDTH, seq), BF16),
                   jax.ShapeDtypeStruct((bsz, 2 * B_WIDTH, seq), BF16),
                   jax.ShapeDtypeStruct((bsz, IDX_HEADS * _IDX_K, seq), BF16),
                   jax.ShapeDtypeStruct((bsz, IDX_HEADS, seq), F32),
                   jax.ShapeDtypeStruct((bsz, seq // KB, _AV_ROWS, KB), BF16),
                   jax.ShapeDtypeStruct((bsz, seq // KB, B_HEADS * _BV_ROWS, KB), BF16),
                   jax.ShapeDtypeStruct((bsz * seq, B_WIDTH), BF16),
                   jax.ShapeDtypeStruct((bsz * seq, A_HEAD_DIM), BF16),
                   jax.ShapeDtypeStruct((bsz * seq, _IDX_K), BF16)],
        scratch_shapes=[pltpu.VMEM((_T_ROWS, tm), F32), pltpu.VMEM((_X_ROWS, tm), F32)],
        compiler_params=pltpu.CompilerParams(vmem_limit_bytes=VMEM_LIMIT),
        name="in_proj",
    )(x2, ln_g, ln_b, wn, wt, wx, cq, sq, ci, si, nk)


def _query_limit(j):
    lane = lax.broadcasted_iota(I32, (1, QB), 1)
    return j * QB + (lane // CHUNK + 1) * CHUNK


def _key_index(off):
    return off + lax.broadcasted_iota(I32, (KB, QB), 0)


def _bit_transpose32(words):
    a = list(words)
    j, m = 16, 0x0000FFFF
    while j:
        mask = m - (1 << 32) if m >= (1 << 31) else m
        for k in range(32):
            if k & j == 0:
                t = (a[k] ^ lax.shift_right_logical(a[k + j], jnp.int32(j))) & mask
                a[k] = a[k] ^ t
                a[k + j] = a[k + j] ^ lax.shift_left(t, jnp.int32(j))
        j >>= 1
        m = (m ^ (m << j)) & 0xFFFFFFFF
    return a


def _online_softmax(last, logits, values, vrows, s_bufs, mx_bufs, m_scr, acc_scr, side=None):
    def stage_logits(kb, buf, diag):
        for mp, s in enumerate(logits(kb, diag)):
            sb = s.astype(BF16)
            s_bufs[buf][mp] = sb
            mx_bufs[buf][mp, 0:1, :] = jnp.max(sb, axis=0, keepdims=True).astype(F32)
            yield

    def stage_values(kb, buf, is_last=False):
        steps = side(kb, is_last) if side is not None else iter(())
        for mp, vt in enumerate(values(kb)):
            next(steps, None)
            rows = slice(vrows * mp, vrows * (mp + 1))
            m_old = m_scr[mp, 0:1, :]
            m_new = jnp.maximum(m_old, mx_bufs[buf][mp, 0:1, :])
            alpha = jnp.exp2(m_old - m_new)
            p = jnp.exp2(s_bufs[buf][mp] - m_new.astype(BF16))
            acc_scr[rows, :] = alpha * acc_scr[rows, :] + _dot(vt, p)
            m_scr[mp, 0:1, :] = m_new
            yield
        for _ in steps:
            pass

    def run(*stages):
        end = object()
        live = list(stages)
        while live:
            if side is None:
                live = [g for g in live if next(g, end) is not end]
            else:
                for _ in live.pop(0):
                    pass

    def pair(t, carry):
        kb = 2 * t
        run(stage_logits(kb + 1, 1, False), stage_values(kb, 0))
        run(stage_logits(kb + 2, 0, False), stage_values(kb + 1, 1))
        return carry

    pl.when(last == 0)(lambda: run(stage_logits(0, 0, True)))
    pl.when(last > 0)(lambda: run(stage_logits(0, 0, False)))
    trips = jnp.maximum(last - 1, 0) // 2
    lax.fori_loop(0, trips, pair, 0)
    base = 2 * trips
    rem = last - base

    @pl.when(rem == 0)
    def _():
        run(stage_values(last, 0, True))

    @pl.when(rem == 1)
    def _():
        run(stage_logits(last, 1, True), stage_values(base, 0))
        run(stage_values(last, 1, True))

    @pl.when(rem == 2)
    def _():
        run(stage_logits(base + 1, 1, False), stage_values(base, 0))
        run(stage_logits(last, 0, True), stage_values(base + 1, 1))
        run(stage_values(last, 0, True))


def _mixers_kernel(n_sel, nqb, lam_init, aqt_ref, iqt_ref, iwt_ref, ak_ref, ik_ref, avt_ref,
                   bqt_ref, bk_ref, bvt_ref, dl_ref, g_ref, oa_ref, ob_ref,
                   keys_scr, planes_scr, thr_scr, s0_scr, s1_scr, mx0_scr, mx1_scr, m_scr, acc_scr):
    s_bufs, mx_bufs = (s0_scr, s1_scr), (mx0_scr, mx1_scr)
    j = pl.program_id(1)
    nkb = j + 1
    limit = _query_limit(j)
    iw = iwt_ref[0]

    def score_block(kb, diag):
        off = pl.multiple_of(kb * KB, KB)
        ikb = ik_ref[0, pl.ds(off, KB), :]
        sc = jnp.zeros((KB, QB), F32)
        for hd in range(IDX_HEADS):
            z = _dot(ikb, iqt_ref[0, _IDX_K * hd:_IDX_K * (hd + 1), :])
            sc = sc + iw[hd:hd + 1, :] * jnp.maximum(z, 0.0)
            yield
        bits = pltpu.bitcast(sc, I32)
        key = bits ^ ((bits >> 31) & 0x7FFFFFFF)
        if diag:
            key = jnp.where(_key_index(off) < limit, key, INT_MIN)
        keys_scr[pl.ds(off, KB), :] = key
        ukey = key ^ INT_MIN
        planes = _bit_transpose32([ukey[SUBLANES * i:SUBLANES * (i + 1)] for i in range(32)])
        for p in range(32):
            planes_scr[kb, p] = planes[p]

    vdim = 2 * B_HEAD_DIM
    nmaps = 2 * B_HEADS
    m_scr[...] = jnp.full(m_scr.shape, NEG, F32)
    acc_scr[...] = jnp.zeros(acc_scr.shape, F32)

    def b_logits(kb, diag):
        off = pl.multiple_of(kb * KB, KB)
        vis = (_key_index(off) < limit) if diag else None
        for mp in range(nmaps):
            kpair = bk_ref[0, pl.ds(off, KB), vdim * (mp // 2):vdim * (mp // 2 + 1)]
            s = _dot(kpair, bqt_ref[0, vdim * mp:vdim * (mp + 1), :])
            yield jnp.where(vis, s, NEG) if diag else s

    def b_values(kb):
        return [bvt_ref[0, kb, _BV_ROWS * (mp // 2):_BV_ROWS * (mp // 2 + 1), :] for mp in range(nmaps)]

    _online_softmax(j, b_logits, b_values, _BV_ROWS, s_bufs, mx_bufs, m_scr, acc_scr, side=score_block)

    dl = dl_ref[...]
    lam = (jnp.exp(jnp.sum(dl[0:1] * dl[1:2], axis=1, keepdims=True))
           - jnp.exp(jnp.sum(dl[2:3] * dl[3:4], axis=1, keepdims=True)) + lam_init)
    gain = g_ref[...] * (1.0 - lam_init)
    outs = []
    for hd in range(B_HEADS):
        b1, b2 = _BV_ROWS * 2 * hd, _BV_ROWS * (2 * hd + 1)
        o1 = acc_scr[b1:b1 + vdim, :] / acc_scr[b1 + vdim:b1 + vdim + 1, :]
        o2 = acc_scr[b2:b2 + vdim, :] / acc_scr[b2 + vdim:b2 + vdim + 1, :]
        o = o1 - lam * o2
        ms = jnp.mean(o * o, axis=0, keepdims=True)
        outs.append(o * lax.rsqrt(ms + LN_EPS) * gain)
    ob_ref[0] = jnp.concatenate(outs, axis=0).T.astype(BF16)


    def select_threshold(nblk):
        def tree_sum(parts):
            while len(parts) > 1:
                parts = [a + b for a, b in zip(parts[0::2], parts[1::2])] + parts[len(parts) & ~1:]
            return jnp.sum(parts[0], axis=0, keepdims=True)

        def step(it, carry):
            need, thr_u = carry[0], carry[1]
            alive = carry[2:]
            ones = [alive[kb] & planes_scr[kb, it] for kb in range(nblk)]
            c1 = tree_sum([lax.population_count(x) for x in ones])
            take = c1 >= need
            need = jnp.where(take, need, need - c1)
            thr_u = thr_u | jnp.where(take, lax.shift_left(jnp.int32(1), 31 - it), 0)
            alive = [jnp.where(take, x, a ^ x) for x, a in zip(ones, alive)]
            return (need, thr_u, *alive)

        init = (jnp.full((1, QB), n_sel, I32), jnp.zeros((1, QB), I32),
                *([jnp.full((SUBLANES, QB), -1, I32)] * nblk))
        out = lax.fori_loop(0, 32, step, init)
        need, thr_u = out[0], out[1]
        equal = tree_sum([lax.population_count(a) for a in out[2:]])
        thr_scr[0:1, :] = thr_u ^ INT_MIN
        thr_scr[1:2, :] = (n_sel - need) + equal

    for k in range(nqb):
        pl.when(j == k)(functools.partial(select_threshold, k + 1))
    thr = thr_scr[0:1, :]
    cnt = thr_scr[1:2, :]

    def count(pred_fn):
        def body(kb, c):
            off = pl.multiple_of(kb * KB, KB)
            hit = pred_fn(keys_scr[pl.ds(off, KB), :])
            return c + jnp.sum(jnp.where(hit, 1.0, 0.0), axis=0, keepdims=True)
        return lax.fori_loop(0, nkb, body, jnp.zeros((1, QB), F32))

    tied = (cnt > n_sel) & (thr > INT_MIN)

    @pl.when(jnp.max(jnp.where(tied, 1.0, 0.0)) > 0.0)
    def _():
        need = n_sel - count(lambda blk: blk > thr)
        thr_tied = jnp.where(tied, thr, INT_MIN)
        tri = (lax.broadcasted_iota(I32, (KB, KB), 0) >= lax.broadcasted_iota(I32, (KB, KB), 1)).astype(BF16)

        def body(kb, seen):
            off = pl.multiple_of(kb * KB, KB)
            blk = keys_scr[pl.ds(off, KB), :]
            eq = blk == thr_tied
            eqb = jnp.where(eq, 1.0, 0.0).astype(BF16)
            rank = _dot(tri, eqb) + seen
            keys_scr[pl.ds(off, KB), :] = jnp.where(eq & (rank > need), INT_MIN, blk)
            return seen + jnp.sum(eqb.astype(F32), axis=0, keepdims=True)

        lax.fori_loop(0, nkb, body, jnp.zeros((1, QB), F32))

    thr = jnp.maximum(thr, INT_MIN + 1)

    m_scr[...] = jnp.full(m_scr.shape, NEG, F32)
    acc_scr[0:A_HEADS * _AV_ROWS, :] = jnp.zeros((A_HEADS * _AV_ROWS, QB), F32)

    def logits(kb, diag):
        del diag
        off = pl.multiple_of(kb * KB, KB)
        sel = keys_scr[pl.ds(off, KB), :] >= thr
        akb = ak_ref[0, pl.ds(off, KB), :]
        for hd in range(A_HEADS):
            s = _dot(akb, aqt_ref[0, A_HEAD_DIM * hd:A_HEAD_DIM * (hd + 1), :])
            yield jnp.where(sel, s, NEG)

    def values(kb):
        avb = avt_ref[0, kb]
        return [avb] * A_HEADS

    _online_softmax(j, logits, values, _AV_ROWS, s_bufs, mx_bufs, m_scr, acc_scr)

    outs = []
    for hd in range(A_HEADS):
        base = _AV_ROWS * hd
        outs.append(acc_scr[base:base + A_HEAD_DIM, :] / acc_scr[base + A_HEAD_DIM:base + A_HEAD_DIM + 1, :])
    oa_ref[0] = jnp.concatenate(outs, axis=0).T.astype(BF16)


def _mixers(aqt, iqt, iwt, ak, ik, avt, bqt, bk, bvt, dl, gcol, lam_init, bsz, seq):
    nqb = seq // QB
    nmaps = 2 * B_HEADS
    assert nmaps == A_HEADS
    qblk = lambda b, j: (b, 0, j)
    full3 = lambda b, j: (b, 0, 0)
    full4 = lambda b, j: (b, 0, 0, 0)
    const2 = lambda b, j: (0, 0)
    oblk = lambda b, j: (b, j, 0)
    n_sel = min(TOPK_MAX, seq // 4)
    return pl.pallas_call(
        functools.partial(_mixers_kernel, n_sel, nqb, lam_init),
        grid=(bsz, nqb),
        in_specs=[pl.BlockSpec((1, A_WIDTH, QB), qblk),
                  pl.BlockSpec((1, IDX_HEADS * _IDX_K, QB), qblk),
                  pl.BlockSpec((1, IDX_HEADS, QB), qblk),
                  pl.BlockSpec((1, seq, A_HEAD_DIM), full3),
                  pl.BlockSpec((1, seq, _IDX_K), full3),
                  pl.BlockSpec((1, seq // KB, _AV_ROWS, KB), full4),
                  pl.BlockSpec((1, 2 * B_WIDTH, QB), qblk),
                  pl.BlockSpec((1, seq, B_WIDTH), full3),
                  pl.BlockSpec((1, seq // KB, B_HEADS * _BV_ROWS, KB), full4),
                  pl.BlockSpec((4, B_HEAD_DIM), const2),
                  pl.BlockSpec((2 * B_HEAD_DIM, 1), const2)],
        out_specs=[pl.BlockSpec((1, QB, A_WIDTH), oblk), pl.BlockSpec((1, QB, B_WIDTH), oblk)],
        out_shape=[jax.ShapeDtypeStruct((bsz, seq, A_WIDTH), BF16), jax.ShapeDtypeStruct((bsz, seq, B_WIDTH), BF16)],
        scratch_shapes=[pltpu.VMEM((seq, QB), I32),
                        pltpu.VMEM((seq // KB, 32, SUBLANES, QB), I32),
                        pltpu.VMEM((8, QB), I32),
                        pltpu.VMEM((nmaps, KB, QB), BF16),
                        pltpu.VMEM((nmaps, KB, QB), BF16),
                        pltpu.VMEM((nmaps, 8, QB), F32),
                        pltpu.VMEM((nmaps, 8, QB), F32),
                        pltpu.VMEM((nmaps, 8, QB), F32),
                        pltpu.VMEM((nmaps * _BV_ROWS, QB), F32)],
        compiler_params=pltpu.CompilerParams(vmem_limit_bytes=VMEM_LIMIT),
        name="mixers",
    )(aqt, iqt, iwt, ak, ik, avt, bqt, bk, bvt, dl, gcol)


_TAIL_ROWS = 256


def _silu(x):
    return x * jax.nn.sigmoid(x)


def _tail_kernel(x_ref, oa_ref, ob_ref, mkt_ref, mv_ref, lig_ref, lib_ref, wcq_ref, wg_ref, wm_ref,
                 wpa_ref, wpb_ref, wpc_ref, wo_ref, lg_ref, lb_ref, out_ref):
    tm = x_ref.shape[0]
    groups = [slice(r, r + _TAIL_ROWS) for r in range(0, tm, _TAIL_ROWS)]
    each = lambda fn, *lists: [fn(*args) for args in zip(*lists)]

    h = [_layer_norm(x_ref[g, :], lig_ref[...], lib_ref[...]) for g in groups]
    hb = each(lambda v: v.astype(BF16), h)
    cq = each(lambda v: _dot(v, wcq_ref[...]).astype(BF16), hb)
    gate_cols = lambda n: slice(A_WIDTH * n, A_WIDTH * (n + 1))
    merge_cols = lambda n: slice(D_MODEL * n, D_MODEL * (n + 1))
    silu_gate = lambda n: each(lambda v: _silu(_dot(v, wg_ref[:, gate_cols(n)])), hb)
    merge_gate = lambda n: each(lambda v: jax.nn.sigmoid(_dot(v, wm_ref[:, merge_cols(n)])), hb)
    u_a = each(lambda g, sg: (oa_ref[0, g, :].astype(F32) * sg).astype(BF16), groups, silu_gate(0))
    u_b = each(lambda g, sg: (ob_ref[0, g, :].astype(F32) * sg).astype(BF16), groups, silu_gate(1))
    sg_c = silu_gate(2)

    heads = [slice(C_HEAD_DIM * hd, C_HEAD_DIM * (hd + 1)) for hd in range(C_HEADS)]

    def mem_probs(q):
        out = []
        for cols in heads:
            s = _dot(q[:, cols], mkt_ref[0, cols, :]) * (C_HEAD_DIM ** -0.5)
            p = jnp.exp(s - jnp.max(s, axis=-1, keepdims=True))
            out.append((p / jnp.sum(p, axis=-1, keepdims=True)).astype(BF16))
        return out

    probs = each(mem_probs, cq)
    mg_a = merge_gate(0)
    o_c = each(lambda ps: jnp.concatenate([_dot(p, mv_ref[0, :, cols]) for p, cols in zip(ps, heads)], axis=1),
               probs)
    merged = each(lambda u, mg: mg * _dot(u, wpa_ref[...]), u_a, mg_a)
    mg_b = merge_gate(1)
    merged = each(lambda m, u, mg: m + mg * _dot(u, wpb_ref[...]), merged, u_b, mg_b)
    mg_c = merge_gate(2)
    merged = each(lambda m, o, sg, mg: m + mg * _dot((o * sg).astype(BF16), wpc_ref[...]), merged, o_c, sg_c, mg_c)
    z = each(lambda hh, m: DEEPNORM_ALPHA * hh + _dot(m.astype(BF16), wo_ref[...]), h, merged)
    for g, zz in zip(groups, z):
        out_ref[g, :] = _layer_norm(zz, lg_ref[...], lb_ref[...])


def _tail(x2, o_a, o_b, mkt, mv, lig, lib, wcq, wg, wm, wpa, wpb, wpc, wo, lg, lb, bsz, seq, tm):
    d = x2.shape[1]
    spb = seq // tm
    mlen = mv.shape[1]
    tok = lambda i: (i, 0)
    tok3 = lambda i: (i // spb, i % spb, 0)
    bat3 = lambda i: (i // spb, 0, 0)
    const2 = lambda i: (0, 0)
    wspec = lambda w: pl.BlockSpec(w.shape, const2, pipeline_mode=pl.Buffered(1))
    vec = pl.BlockSpec((1, d), const2)
    return pl.pallas_call(
        _tail_kernel,
        grid=(bsz * spb,),
        in_specs=[pl.BlockSpec((tm, d), tok),
                  pl.BlockSpec((1, tm, A_WIDTH), tok3),
                  pl.BlockSpec((1, tm, B_WIDTH), tok3),
                  pl.BlockSpec((1, C_WIDTH, mlen), bat3),
                  pl.BlockSpec((1, mlen, C_WIDTH), bat3),
                  vec, vec, wspec(wcq), wspec(wg), wspec(wm), wspec(wpa), wspec(wpb), wspec(wpc), wspec(wo),
                  vec, vec],
        out_specs=pl.BlockSpec((tm, d), tok),
        out_shape=jax.ShapeDtypeStruct((bsz * seq, d), F32),
        compiler_params=pltpu.CompilerParams(vmem_limit_bytes=VMEM_LIMIT),
        name="tail",
    )(x2, o_a, o_b, mkt, mv, lig, lib, wcq, wg, wm, wpa, wpb, wpc, wo, lg, lb)


def _rope_tables(seq):
    pos = jnp.arange(seq, dtype=F32)

    def cos_sin(dim):
        r = dim // ROPE_FRACTION
        half = r // 2
        inv = jnp.power(ROPE_THETA, -jnp.arange(half, dtype=F32) * (2.0 / r))
        ang = pos[:, None] * inv[None, :]
        return jnp.cos(ang), jnp.sin(ang), half

    c64, s64, h64 = cos_sin(A_HEAD_DIM)
    c32, s32, h32 = cos_sin(IDX_DIM)
    cq, sq = c64.T, s64.T
    ci = jnp.concatenate([c32.T, c32.T], axis=0)
    si = jnp.concatenate([-s32.T, s32.T], axis=0)

    def natural(cos, sin, half, dim, width):
        zero = jnp.zeros((seq, dim - 2 * half), F32)
        c = jnp.concatenate([cos, cos, jnp.ones_like(zero)], axis=1)
        a = jnp.concatenate([jnp.zeros_like(sin), sin, zero], axis=1)
        b = jnp.concatenate([-sin, jnp.zeros_like(sin), zero], axis=1)
        rep = width // dim
        return [jnp.tile(t, (1, rep)) for t in (c, a, b)]

    nk = jnp.stack(natural(c64, s64, h64, A_HEAD_DIM, LANES))
    return cq, sq, ci, si, nk


def kernel(x, mem, ln_in_g, ln_in_b, w_in, w_mem_kv, diff_lambda, diff_norm_g,
           w_proj_a, w_proj_b, w_proj_c, w_out, ln_g, ln_b):
    bsz, seq, d = x.shape
    assert d == D_MODEL and seq % QB == 0 and w_in.shape[0] == DEPTH == 1
    tm_proj = 512 if seq % 512 == 0 else QB
    tm_tail = 512 if seq % 512 == 0 else _TAIL_ROWS
    lam_init = 0.8 - 0.6 * math.exp(-0.3 * 0)

    offs = np.cumsum((0,) + IN_SPLITS)
    w_t = w_in[0].T
    w_tb = w_t.astype(BF16)
    (r_aq, r_ak, r_av, r_ag, r_iq, r_ik, r_iw, r_bq, r_bk, r_bv, r_bg, r_cq, r_cg, r_mg) = [
        slice(offs[i], offs[i + 1]) for i in range(len(IN_SPLITS))]
    wt = jnp.concatenate([(w_t[r_aq] * (A_HEAD_DIM ** -0.5 * LOG2E)).astype(BF16),
                          (w_t[r_bq] * (B_HEAD_DIM ** -0.5 * LOG2E)).astype(BF16),
                          w_tb[r_bv], w_tb[r_av]], axis=0)
    wx = jnp.concatenate([w_t[r_iq], w_t[r_ik], w_t[r_iw] * ((IDX_HEADS * IDX_DIM) ** -0.5),
                          jnp.zeros((_X_ROWS - _X_IW - IDX_HEADS, d), F32)], axis=0)
    wx_hi = wx.astype(BF16)
    wx_mid = (wx - wx_hi.astype(F32)).astype(BF16)
    wx_lo = (wx - wx_hi.astype(F32) - wx_mid.astype(F32)).astype(BF16)
    wx = jnp.concatenate([wx_hi, wx_mid, wx_lo], axis=0)
    wn = jnp.concatenate([w_tb[r_bk], w_tb[r_ak], jnp.zeros((LANES - A_HEAD_DIM, d), BF16)], axis=0).T
    wg = jnp.concatenate([w_tb[r_ag], w_tb[r_bg], w_tb[r_cg]], axis=0).T
    w_cq, w_mg = w_tb[r_cq].T, w_tb[r_mg].T
    row = lambda v: v.reshape(1, -1).astype(F32)

    x2 = x.reshape(bsz * seq, d)
    tabs = _rope_tables(seq)
    mkt, mv = _mem_kv(mem, w_mem_kv[0][:, :C_WIDTH].T.astype(BF16), w_mem_kv[0][:, C_WIDTH:].astype(BF16))
    aqt, bqt, iqt, iwt, avt, bvt, bk, ak, ik = _in_proj(
        x2, row(ln_in_g), row(ln_in_b), wn, wt, wx, tabs, bsz, seq, tm_proj)
    o_a, o_b = _mixers(aqt, iqt, iwt, ak.reshape(bsz, seq, A_HEAD_DIM), ik.reshape(bsz, seq, _IDX_K), avt,
                       bqt, bk.reshape(bsz, seq, B_WIDTH), bvt, diff_lambda[0].astype(F32),
                       diff_norm_g[0].reshape(-1, 1).astype(F32), lam_init, bsz, seq)
    out = _tail(x2, o_a, o_b, mkt, mv, row(ln_in_g), row(ln_in_b), w_cq, wg, w_mg,
                w_proj_a[0].astype(BF16), w_proj_b[0].astype(BF16), w_proj_c[0].astype(BF16),
                w_out[0].astype(BF16), row(ln_g[0]), row(ln_b[0]), bsz, seq, tm_tail)
    return out.reshape(bsz, seq, d)
```

```python
import functools
import itertools
import math

import numpy as np
import jax
import jax.numpy as jnp
from jax import lax
from jax.experimental import pallas as pl
from jax.experimental.pallas import tpu as pltpu

D_MODEL = 1024
CHUNK = 64
ROPE_THETA = 500000.0
ROPE_FRACTION = 4
LN_EPS = 1e-5
A_HEADS = 8
A_HEAD_DIM = 64
A_WIDTH = A_HEADS * A_HEAD_DIM
IDX_HEADS = 8
IDX_DIM = 32
TOPK_MAX = 256
B_HEADS = 4
B_HEAD_DIM = 64
B_WIDTH = B_HEADS * 2 * B_HEAD_DIM
C_HEADS = 4
C_HEAD_DIM = 128
C_WIDTH = C_HEADS * C_HEAD_DIM
N_BRANCH = 3
DEPTH = 1
DEEPNORM_ALPHA = (2.0 * DEPTH) ** 0.25
IN_SPLITS = (A_WIDTH, A_HEAD_DIM, A_HEAD_DIM, A_WIDTH, IDX_HEADS * IDX_DIM, IDX_DIM, IDX_HEADS,
             B_WIDTH, B_WIDTH, B_WIDTH, B_WIDTH, C_WIDTH, C_WIDTH, N_BRANCH * D_MODEL)

LANES = 128
QB = 256
KB = 256
NEG = -1e30
INT_MIN = -2 ** 31
VMEM_LIMIT = 56 * 1024 * 1024

BF16 = jnp.bfloat16
F32 = jnp.float32
I32 = jnp.int32
SUBLANES = 8
assert KB == 32 * SUBLANES


def _nt_dot(a, b):
    return lax.dot_general(a, b, (((1,), (1,)), ((), ())), preferred_element_type=F32)


def _dot(a, b):
    return jnp.dot(a, b, preferred_element_type=F32)


def _layer_norm(x, g, b):
    mu = jnp.mean(x, axis=-1, keepdims=True)
    xc = x - mu
    var = jnp.mean(xc * xc, axis=-1, keepdims=True)
    return xc * lax.rsqrt(var + LN_EPS) * g + b


def _mem_kv_kernel(mem_ref, wkt_ref, wv_ref, mkt_ref, mv_ref):
    mb = mem_ref[0].astype(BF16)
    mkt_ref[0] = _nt_dot(wkt_ref[...], mb).astype(BF16)
    mv_ref[0] = _dot(mb, wv_ref[...]).astype(BF16)


def _mem_kv(mem, wkt, wv):
    bsz, mlen, d = mem.shape
    return pl.pallas_call(
        _mem_kv_kernel,
        grid=(bsz,),
        in_specs=[pl.BlockSpec((1, mlen, d), lambda b: (b, 0, 0)),
                  pl.BlockSpec((C_WIDTH, d), lambda b: (0, 0)),
                  pl.BlockSpec((d, C_WIDTH), lambda b: (0, 0))],
        out_specs=[pl.BlockSpec((1, C_WIDTH, mlen), lambda b: (b, 0, 0)),
                   pl.BlockSpec((1, mlen, C_WIDTH), lambda b: (b, 0, 0))],
        out_shape=[jax.ShapeDtypeStruct((bsz, C_WIDTH, mlen), BF16),
                   jax.ShapeDtypeStruct((bsz, mlen, C_WIDTH), BF16)],
        name="mem_kv",
    )(mem, wkt, wv)


_T_AQ = 0
_T_BQ = _T_AQ + A_WIDTH
_T_BV = _T_BQ + B_WIDTH
_T_AV = _T_BV + B_WIDTH
_T_ROWS = _T_AV + A_HEAD_DIM
_X_IQ = 0
_X_IK = _X_IQ + IDX_HEADS * IDX_DIM
_X_IW = _X_IK + IDX_DIM
_X_ROWS = _X_IW + 16
_IDX_K = 256
_N_COLS = B_WIDTH + LANES
_ONES_ROWS = 16
_AV_ROWS = A_HEAD_DIM + _ONES_ROWS
_BV_ROWS = 2 * B_HEAD_DIM + _ONES_ROWS
LOG2E = math.log2(math.e)


def _rope_rows64(r, cos, sin):
    x1, x2 = r[0:8], r[8:16]
    return jnp.concatenate([x1 * cos - x2 * sin, x2 * cos + x1 * sin, r[16:]], axis=0)


def _rope_rows32(r, cos2, sin2):
    top = r[0:8]
    return jnp.concatenate([top * cos2 + pltpu.roll(top, 4, 0) * sin2, r[8:]], axis=0)


def _split3(x):
    hi = x.astype(BF16)
    r = x - hi.astype(F32)
    mid = r.astype(BF16)
    lo = (r - mid.astype(F32)).astype(BF16)
    return hi, mid, lo


def _in_proj_kernel(x_ref, g_ref, b_ref, wn_ref, wt_ref, wx_ref, cq_ref, sq_ref, ci_ref, si_ref, nk_ref,
                    aqt_ref, bqt_ref, iqt_ref, iwt_ref, avt_ref, bvt_ref, bk_ref, ak_ref, ik_ref, rt_scr, xt_scr):
    tm = x_ref.shape[0]
    h = _layer_norm(x_ref[...], g_ref[...], b_ref[...])
    h_hi, h_mid, h_lo = _split3(h)
    hb = h_hi

    kn = _dot(hb, wn_ref[...])
    c64, a64, b64 = nk_ref[0], nk_ref[1], nk_ref[2]
    for g in range(_N_COLS // LANES):
        v = kn[:, g * LANES:(g + 1) * LANES]
        v = (v * c64 + pltpu.roll(v, 8, 1) * a64 + pltpu.roll(v, LANES - 8, 1) * b64).astype(BF16)
        if g < B_WIDTH // LANES:
            bk_ref[:, g * LANES:(g + 1) * LANES] = v
        else:
            ak_ref[...] = v[:, :A_HEAD_DIM]

    n = _X_ROWS
    t1 = _nt_dot(wx_ref[...], h_hi)
    t2 = _nt_dot(wx_ref[0:2 * n, :], h_mid)
    t3 = _nt_dot(wx_ref[0:n, :], h_lo)
    xt_scr[...] = (t3 + t2[n:2 * n] + t1[2 * n:3 * n]) + (t2[0:n] + t1[n:2 * n]) + t1[0:n]
    ci, si = ci_ref[...], si_ref[...]
    zpad = jnp.zeros((_IDX_K - 6 * IDX_DIM, tm), BF16)
    for hd in range(IDX_HEADS):
        qh, qm, ql = _split3(_rope_rows32(xt_scr[_X_IQ + IDX_DIM * hd:_X_IQ + IDX_DIM * (hd + 1), :], ci, si))
        iqt_ref[0, _IDX_K * hd:_IDX_K * (hd + 1), :] = jnp.concatenate([qh, qm, ql, qh, qm, qh, zpad], axis=0)
    kh, km, kl = [t.astype(F32) for t in _split3(_rope_rows32(xt_scr[_X_IK:_X_IK + IDX_DIM, :], ci, si))]
    k6 = jnp.concatenate([kh, kh, kh, km, km, kl, zpad.astype(F32)], axis=0)
    ik_ref[...] = k6.T.astype(BF16)
    iwt_ref[0] = xt_scr[_X_IW:_X_IW + IDX_HEADS, :]

    rt_scr[...] = _nt_dot(wt_ref[...], hb)
    cos, sin = cq_ref[...], sq_ref[...]
    for hd in range(A_HEADS):
        r = rt_scr[_T_AQ + 64 * hd:_T_AQ + 64 * (hd + 1), :]
        aqt_ref[0, 64 * hd:64 * (hd + 1), :] = _rope_rows64(r, cos, sin).astype(BF16)
    zeros = jnp.zeros((B_HEAD_DIM, tm), BF16)
    for mp in range(2 * B_HEADS):
        r = _rope_rows64(rt_scr[_T_BQ + 64 * mp:_T_BQ + 64 * (mp + 1), :], cos, sin).astype(BF16)
        bqt_ref[0, 128 * mp:128 * (mp + 1), :] = jnp.concatenate([r, zeros] if mp % 2 == 0 else [zeros, r], axis=0)
    ones = jnp.ones((_ONES_ROWS, KB), BF16)
    vdim = 2 * B_HEAD_DIM
    for t in range(tm // KB):
        cols = slice(t * KB, (t + 1) * KB)
        for hd in range(B_HEADS):
            bvt_ref[0, t, _BV_ROWS * hd:_BV_ROWS * hd + vdim, :] = (
                rt_scr[_T_BV + vdim * hd:_T_BV + vdim * (hd + 1), cols].astype(BF16))
            bvt_ref[0, t, _BV_ROWS * hd + vdim:_BV_ROWS * (hd + 1), :] = ones
        avt_ref[0, t, 0:A_HEAD_DIM, :] = rt_scr[_T_AV:_T_AV + A_HEAD_DIM, cols].astype(BF16)
        avt_ref[0, t, A_HEAD_DIM:_AV_ROWS, :] = ones


def _in_proj(x2, ln_g, ln_b, wn, wt, wx, tabs, bsz, seq, tm):
    d = x2.shape[1]
    spb = seq // tm
    tpk = tm // KB
    cq, sq, ci, si, nk = tabs
    tok = lambda i: (i, 0)
    tokt = lambda i: (i // spb, 0, i % spb)
    const2 = lambda i: (0, 0)
    return pl.pallas_call(
        _in_proj_kernel,
        grid=(bsz * spb,),
        in_specs=[pl.BlockSpec((tm, d), tok),
                  pl.BlockSpec((1, d), const2), pl.BlockSpec((1, d), const2),
                  pl.BlockSpec(wn.shape, const2), pl.BlockSpec(wt.shape, const2), pl.BlockSpec(wx.shape, const2),
                  pl.BlockSpec((8, tm), lambda i: (0, i % spb)), pl.BlockSpec((8, tm), lambda i: (0, i % spb)),
                  pl.BlockSpec((8, tm), lambda i: (0, i % spb)), pl.BlockSpec((8, tm), lambda i: (0, i % spb)),
                  pl.BlockSpec((3, tm, LANES), lambda i: (0, i % spb, 0))],
        out_specs=[pl.BlockSpec((1, A_WIDTH, tm), tokt),
                   pl.BlockSpec((1, 2 * B_WIDTH, tm), tokt),
                   pl.BlockSpec((1, IDX_HEADS * _IDX_K, tm), tokt),
                   pl.BlockSpec((1, IDX_HEADS, tm), tokt),
                   pl.BlockSpec((1, tpk, _AV_ROWS, KB), lambda i: (i // spb, i % spb, 0, 0)),
                   pl.BlockSpec((1, tpk, B_HEADS * _BV_ROWS, KB), lambda i: (i // spb, i % spb, 0, 0)),
                   pl.BlockSpec((tm, B_WIDTH), tok),
                   pl.BlockSpec((tm, A_HEAD_DIM), tok),
                   pl.BlockSpec((tm, _IDX_K), tok)],
        out_shape=[jax.ShapeDtypeStruct((bsz, A_WIDTH, seq), BF16),
                   jax.ShapeDtypeStruct((bsz, 2 * B_WIDTH, seq), BF16),
                   jax.ShapeDtypeStruct((bsz, IDX_HEADS * _IDX_K, seq), BF16),
                   jax.ShapeDtypeStruct((bsz, IDX_HEADS, seq), F32),
                   jax.ShapeDtypeStruct((bsz, seq // KB, _AV_ROWS, KB), BF16),
                   jax.ShapeDtypeStruct((bsz, seq // KB, B_HEADS * _BV_ROWS, KB), BF16),
                   jax.ShapeDtypeStruct((bsz * seq, B_WIDTH), BF16),
                   jax.ShapeDtypeStruct((bsz * seq, A_HEAD_DIM), BF16),
                   jax.ShapeDtypeStruct((bsz * seq, _IDX_K), BF16)],
        scratch_shapes=[pltpu.VMEM((_T_ROWS, tm), F32), pltpu.VMEM((_X_ROWS, tm), F32)],
        compiler_params=pltpu.CompilerParams(vmem_limit_bytes=VMEM_LIMIT),
        name="in_proj",
    )(x2, ln_g, ln_b, wn, wt, wx, cq, sq, ci, si, nk)


def _query_limit(j):
    lane = lax.broadcasted_iota(I32, (1, QB), 1)
    return j * QB + (lane // CHUNK + 1) * CHUNK


def _key_index(off):
    return off + lax.broadcasted_iota(I32, (KB, QB), 0)


def _bit_transpose32(words):
    a = list(words)
    j, m = 16, 0x0000FFFF
    while j:
        mask = m - (1 << 32) if m >= (1 << 31) else m
        for k in range(32):
            if k & j == 0:
                t = (a[k] ^ lax.shift_right_logical(a[k + j], jnp.int32(j))) & mask
                a[k] = a[k] ^ t
                a[k + j] = a[k + j] ^ lax.shift_left(t, jnp.int32(j))
        j >>= 1
        m = (m ^ (m << j)) & 0xFFFFFFFF
    return a


def _online_softmax(last, logits, mask, values, vrows, s_bufs, mx_bufs, m_scr, acc_scr, side=None,
                    block0_stored=False):
    def stage_logits(kb, buf, diag, stored=False):
        apply_mask = mask(kb, diag)
        maps = (s_bufs[buf][mp] for mp in range(m_scr.shape[0])) if stored else logits(kb)
        for mp, s in enumerate(maps):
            sb = apply_mask(s.astype(BF16))
            s_bufs[buf][mp] = sb
            mx_bufs[buf][mp, 0:1, :] = jnp.max(sb, axis=0, keepdims=True).astype(F32)
            yield

    def stage_values(kb, buf, is_last=False):
        steps = side(kb, is_last) if side is not None else iter(())
        for mp, vt in enumerate(values(kb)):
            next(steps, None)
            rows = slice(vrows * mp, vrows * (mp + 1))
            m_old = m_scr[mp, 0:1, :]
            m_new = jnp.maximum(m_old, mx_bufs[buf][mp, 0:1, :])
            alpha = jnp.exp2(m_old - m_new)
            p = jnp.exp2(s_bufs[buf][mp] - m_new.astype(BF16))
            acc_scr[rows, :] = alpha * acc_scr[rows, :] + _dot(vt, p)
            m_scr[mp, 0:1, :] = m_new
            yield
        for _ in steps:
            pass

    def run(*stages):
        end = object()
        live = list(stages)
        while live:
            if side is None:
                live = [g for g in live if next(g, end) is not end]
            else:
                for _ in live.pop(0):
                    pass

    def pair(t, carry):
        kb = 2 * t
        run(stage_logits(kb + 1, 1, False), stage_values(kb, 0))
        run(stage_logits(kb + 2, 0, False), stage_values(kb + 1, 1))
        return carry

    pl.when(last == 0)(lambda: run(stage_logits(0, 0, True, block0_stored)))
    pl.when(last > 0)(lambda: run(stage_logits(0, 0, False, block0_stored)))
    trips = jnp.maximum(last - 1, 0) // 2
    lax.fori_loop(0, trips, pair, 0)
    base = 2 * trips
    rem = last - base

    @pl.when(rem == 0)
    def _():
        run(stage_values(last, 0, True))

    @pl.when(rem == 1)
    def _():
        run(stage_logits(last, 1, True), stage_values(base, 0))
        run(stage_values(last, 1, True))

    @pl.when(rem == 2)
    def _():
        run(stage_logits(base + 1, 1, False), stage_values(base, 0))
        run(stage_logits(last, 0, True), stage_values(base + 1, 1))
        run(stage_values(last, 0, True))


def _mixers_kernel(n_sel, nqb, lam_init, aqt_ref, iqt_ref, iwt_ref, ak_ref, ik_ref, avt_ref,
                   bqt_ref, bk_ref, bvt_ref, dl_ref, g_ref, oa_ref, ob_ref,
                   keys_scr, planes_scr, thr_scr, s0_scr, s1_scr, mx0_scr, mx1_scr, m_scr, acc_scr):
    s_bufs, mx_bufs = (s0_scr, s1_scr), (mx0_scr, mx1_scr)
    j = pl.program_id(1)
    nkb = j + 1
    limit = _query_limit(j)
    iw = iwt_ref[0]

    def score_block(kb, diag):
        off = pl.multiple_of(kb * KB, KB)
        ikb = ik_ref[0, pl.ds(off, KB), :]
        sc = jnp.zeros((KB, QB), F32)
        for hd in range(IDX_HEADS):
            z = _dot(ikb, iqt_ref[0, _IDX_K * hd:_IDX_K * (hd + 1), :])
            sc = sc + iw[hd:hd + 1, :] * jnp.maximum(z, 0.0)
            yield
        bits = pltpu.bitcast(sc, I32)
        key = bits ^ ((bits >> 31) & 0x7FFFFFFF)
        if diag:
            key = jnp.where(_key_index(off) < limit, key, INT_MIN)
        keys_scr[pl.ds(off, KB), :] = key
        ukey = key ^ INT_MIN
        planes = _bit_transpose32([ukey[SUBLANES * i:SUBLANES * (i + 1)] for i in range(32)])
        for p in range(32):
            planes_scr[kb, p] = planes[p]

    vdim = 2 * B_HEAD_DIM
    nmaps = 2 * B_HEADS
    m_scr[...] = jnp.full(m_scr.shape, NEG, F32)
    acc_scr[...] = jnp.zeros(acc_scr.shape, F32)

    def keep_where(keep):
        keep16 = jnp.where(keep, 1, 0).astype(jnp.int16) != 0
        return lambda sb: jnp.where(keep16, sb, NEG)

    def b_logits(kb):
        off = pl.multiple_of(kb * KB, KB)
        for mp in range(nmaps):
            kpair = bk_ref[0, pl.ds(off, KB), vdim * (mp // 2):vdim * (mp // 2 + 1)]
            yield _dot(kpair, bqt_ref[0, vdim * mp:vdim * (mp + 1), :])

    def b_mask(kb, diag):
        if not diag:
            return lambda sb: sb
        return keep_where(_key_index(pl.multiple_of(kb * KB, KB)) < limit)

    def b_values(kb):
        return [bvt_ref[0, kb, _BV_ROWS * (mp // 2):_BV_ROWS * (mp // 2 + 1), :] for mp in range(nmaps)]

    _online_softmax(j, b_logits, b_mask, b_values, _BV_ROWS, s_bufs, mx_bufs, m_scr, acc_scr, side=score_block)

    def b_finish():
        dl = dl_ref[...]
        lam = (jnp.exp(jnp.sum(dl[0:1] * dl[1:2], axis=1, keepdims=True))
               - jnp.exp(jnp.sum(dl[2:3] * dl[3:4], axis=1, keepdims=True)) + lam_init)
        gain = g_ref[...] * (1.0 - lam_init)
        for hd in range(B_HEADS):
            b1, b2 = _BV_ROWS * 2 * hd, _BV_ROWS * (2 * hd + 1)
            o1 = acc_scr[b1:b1 + vdim, :] / acc_scr[b1 + vdim:b1 + vdim + 1, :]
            o2 = acc_scr[b2:b2 + vdim, :] / acc_scr[b2 + vdim:b2 + vdim + 1, :]
            o = o1 - lam * o2
            ms = jnp.mean(o * o, axis=0, keepdims=True)
            ob_ref[0, :, vdim * hd:vdim * (hd + 1)] = (o * lax.rsqrt(ms + LN_EPS) * gain).T.astype(BF16)
            yield

    def a_logits(kb):
        akb = ak_ref[0, pl.ds(pl.multiple_of(kb * KB, KB), KB), :]
        for hd in range(A_HEADS):
            yield _dot(akb, aqt_ref[0, A_HEAD_DIM * hd:A_HEAD_DIM * (hd + 1), :])

    def a_prefill():
        for hd, s in enumerate(a_logits(0)):
            s0_scr[hd] = s.astype(BF16)
            yield


    def select_threshold(nblk):
        def tree_sum(parts):
            while len(parts) > 1:
                parts = [a + b for a, b in zip(parts[0::2], parts[1::2])] + parts[len(parts) & ~1:]
            return jnp.sum(parts[0], axis=0, keepdims=True)

        def step(it, carry):
            need, thr_u = carry[0], carry[1]
            alive = carry[2:]
            ones = [alive[kb] & planes_scr[kb, it] for kb in range(nblk)]
            c1 = tree_sum([lax.population_count(x) for x in ones])
            take = c1 >= need
            need = jnp.where(take, need, need - c1)
            thr_u = thr_u | jnp.where(take, lax.shift_left(jnp.int32(1), 31 - it), 0)
            alive = [jnp.where(take, x, a ^ x) for x, a in zip(ones, alive)]
            return (need, thr_u, *alive)

        out = (jnp.full((1, QB), n_sel, I32), jnp.zeros((1, QB), I32),
               *([jnp.full((SUBLANES, QB), -1, I32)] * nblk))
        filler = itertools.chain(b_finish(), a_prefill())
        for it in range(32):
            next(filler, None)
            out = step(it, out)
        for _ in filler:
            pass
        need, thr_u = out[0], out[1]
        equal = tree_sum([lax.population_count(a) for a in out[2:]])
        thr_scr[0:1, :] = thr_u ^ INT_MIN
        thr_scr[1:2, :] = (n_sel - need) + equal

    for k in range(nqb):
        pl.when(j == k)(functools.partial(select_threshold, k + 1))
    thr = thr_scr[0:1, :]
    cnt = thr_scr[1:2, :]

    def count(pred_fn):
        def body(kb, c):
            off = pl.multiple_of(kb * KB, KB)
            hit = pred_fn(keys_scr[pl.ds(off, KB), :])
            return c + jnp.sum(jnp.where(hit, 1.0, 0.0), axis=0, keepdims=True)
        return lax.fori_loop(0, nkb, body, jnp.zeros((1, QB), F32))

    tied = (cnt > n_sel) & (thr > INT_MIN)

    @pl.when(jnp.max(jnp.where(tied, 1.0, 0.0)) > 0.0)
    def _():
        need = n_sel - count(lambda blk: blk > thr)
        thr_tied = jnp.where(tied, thr, INT_MIN)
        tri = (lax.broadcasted_iota(I32, (KB, KB), 0) >= lax.broadcasted_iota(I32, (KB, KB), 1)).astype(BF16)

        def body(kb, seen):
            off = pl.multiple_of(kb * KB, KB)
            blk = keys_scr[pl.ds(off, KB), :]
            eq = blk == thr_tied
            eqb = jnp.where(eq, 1.0, 0.0).astype(BF16)
            rank = _dot(tri, eqb) + seen
            keys_scr[pl.ds(off, KB), :] = jnp.where(eq & (rank > need), INT_MIN, blk)
            return seen + jnp.sum(eqb.astype(F32), axis=0, keepdims=True)

        lax.fori_loop(0, nkb, body, jnp.zeros((1, QB), F32))

    thr = jnp.maximum(thr, INT_MIN + 1)

    m_scr[...] = jnp.full(m_scr.shape, NEG, F32)
    acc_scr[0:A_HEADS * _AV_ROWS, :] = jnp.zeros((A_HEADS * _AV_ROWS, QB), F32)

    def a_mask(kb, diag):
        del diag
        return keep_where(keys_scr[pl.ds(pl.multiple_of(kb * KB, KB), KB), :] >= thr)

    def a_values(kb):
        avb = avt_ref[0, kb]
        return [avb] * A_HEADS

    _online_softmax(j, a_logits, a_mask, a_values, _AV_ROWS, s_bufs, mx_bufs, m_scr, acc_scr, block0_stored=True)

    outs = []
    for hd in range(A_HEADS):
        base = _AV_ROWS * hd
        outs.append(acc_scr[base:base + A_HEAD_DIM, :] / acc_scr[base + A_HEAD_DIM:base + A_HEAD_DIM + 1, :])
    oa_ref[0] = jnp.concatenate(outs, axis=0).T.astype(BF16)


def _mixers(aqt, iqt, iwt, ak, ik, avt, bqt, bk, bvt, dl, gcol, lam_init, bsz, seq):
    nqb = seq // QB
    nmaps = 2 * B_HEADS
    assert nmaps == A_HEADS
    qblk = lambda b, j: (b, 0, j)
    full3 = lambda b, j: (b, 0, 0)
    full4 = lambda b, j: (b, 0, 0, 0)
    const2 = lambda b, j: (0, 0)
    oblk = lambda b, j: (b, j, 0)
    n_sel = min(TOPK_MAX, seq // 4)
    return pl.pallas_call(
        functools.partial(_mixers_kernel, n_sel, nqb, lam_init),
        grid=(bsz, nqb),
        in_specs=[pl.BlockSpec((1, A_WIDTH, QB), qblk),
                  pl.BlockSpec((1, IDX_HEADS * _IDX_K, QB), qblk),
                  pl.BlockSpec((1, IDX_HEADS, QB), qblk),
                  pl.BlockSpec((1, seq, A_HEAD_DIM), full3),
                  pl.BlockSpec((1, seq, _IDX_K), full3),
                  pl.BlockSpec((1, seq // KB, _AV_ROWS, KB), full4),
                  pl.BlockSpec((1, 2 * B_WIDTH, QB), qblk),
                  pl.BlockSpec((1, seq, B_WIDTH), full3),
                  pl.BlockSpec((1, seq // KB, B_HEADS * _BV_ROWS, KB), full4),
                  pl.BlockSpec((4, B_HEAD_DIM), const2),
                  pl.BlockSpec((2 * B_HEAD_DIM, 1), const2)],
        out_specs=[pl.BlockSpec((1, QB, A_WIDTH), oblk), pl.BlockSpec((1, QB, B_WIDTH), oblk)],
        out_shape=[jax.ShapeDtypeStruct((bsz, seq, A_WIDTH), BF16), jax.ShapeDtypeStruct((bsz, seq, B_WIDTH), BF16)],
        scratch_shapes=[pltpu.VMEM((seq, QB), I32),
                        pltpu.VMEM((seq // KB, 32, SUBLANES, QB), I32),
                        pltpu.VMEM((8, QB), I32),
                        pltpu.VMEM((nmaps, KB, QB), BF16),
                        pltpu.VMEM((nmaps, KB, QB), BF16),
                        pltpu.VMEM((nmaps, 8, QB), F32),
                        pltpu.VMEM((nmaps, 8, QB), F32),
                        pltpu.VMEM((nmaps, 8, QB), F32),
                        pltpu.VMEM((nmaps * _BV_ROWS, QB), F32)],
        compiler_params=pltpu.CompilerParams(vmem_limit_bytes=VMEM_LIMIT),
        name="mixers",
    )(aqt, iqt, iwt, ak, ik, avt, bqt, bk, bvt, dl, gcol)


_TAIL_ROWS = 256


def _silu(x):
    return x * jax.nn.sigmoid(x)


def _tail_kernel(x_ref, oa_ref, ob_ref, mkt_ref, mv_ref, lig_ref, lib_ref, wcq_ref, wg_ref, wm_ref,
                 wpa_ref, wpb_ref, wpc_ref, wo_ref, lg_ref, lb_ref, out_ref):
    tm = x_ref.shape[0]
    groups = [slice(r, r + _TAIL_ROWS) for r in range(0, tm, _TAIL_ROWS)]
    each = lambda fn, *lists: [fn(*args) for args in zip(*lists)]

    h = [_layer_norm(x_ref[g, :], lig_ref[...], lib_ref[...]) for g in groups]
    hb = each(lambda v: v.astype(BF16), h)
    cq = each(lambda v: _dot(v, wcq_ref[...]).astype(BF16), hb)
    gate_cols = lambda n: slice(A_WIDTH * n, A_WIDTH * (n + 1))
    merge_cols = lambda n: slice(D_MODEL * n, D_MODEL * (n + 1))
    silu_gate = lambda n: each(lambda v: _silu(_dot(v, wg_ref[:, gate_cols(n)])), hb)
    merge_gate = lambda n: each(lambda v: jax.nn.sigmoid(_dot(v, wm_ref[:, merge_cols(n)])), hb)
    u_a = each(lambda g, sg: (oa_ref[0, g, :].astype(F32) * sg).astype(BF16), groups, silu_gate(0))
    u_b = each(lambda g, sg: (ob_ref[0, g, :].astype(F32) * sg).astype(BF16), groups, silu_gate(1))
    sg_c = silu_gate(2)

    heads = [slice(C_HEAD_DIM * hd, C_HEAD_DIM * (hd + 1)) for hd in range(C_HEADS)]

    def mem_probs(q):
        out = []
        for cols in heads:
            s = _dot(q[:, cols], mkt_ref[0, cols, :]) * (C_HEAD_DIM ** -0.5)
            p = jnp.exp(s - jnp.max(s, axis=-1, keepdims=True))
            out.append((p / jnp.sum(p, axis=-1, keepdims=True)).astype(BF16))
        return out

    probs = each(mem_probs, cq)
    mg_a = merge_gate(0)
    o_c = each(lambda ps: jnp.concatenate([_dot(p, mv_ref[0, :, cols]) for p, cols in zip(ps, heads)], axis=1),
               probs)
    merged = each(lambda u, mg: mg * _dot(u, wpa_ref[...]), u_a, mg_a)
    mg_b = merge_gate(1)
    merged = each(lambda m, u, mg: m + mg * _dot(u, wpb_ref[...]), merged, u_b, mg_b)
    mg_c = merge_gate(2)
    merged = each(lambda m, o, sg, mg: m + mg * _dot((o * sg).astype(BF16), wpc_ref[...]), merged, o_c, sg_c, mg_c)
    z = each(lambda hh, m: DEEPNORM_ALPHA * hh + _dot(m.astype(BF16), wo_ref[...]), h, merged)
    for g, zz in zip(groups, z):
        out_ref[g, :] = _layer_norm(zz, lg_ref[...], lb_ref[...])


def _tail(x2, o_a, o_b, mkt, mv, lig, lib, wcq, wg, wm, wpa, wpb, wpc, wo, lg, lb, bsz, seq, tm):
    d = x2.shape[1]
    spb = seq // tm
    mlen = mv.shape[1]
    tok = lambda i: (i, 0)
    tok3 = lambda i: (i // spb, i % spb, 0)
    bat3 = lambda i: (i // spb, 0, 0)
    const2 = lambda i: (0, 0)
    wspec = lambda w: pl.BlockSpec(w.shape, const2, pipeline_mode=pl.Buffered(1))
    vec = pl.BlockSpec((1, d), const2)
    return pl.pallas_call(
        _tail_kernel,
        grid=(bsz * spb,),
        in_specs=[pl.BlockSpec((tm, d), tok),
                  pl.BlockSpec((1, tm, A_WIDTH), tok3),
                  pl.BlockSpec((1, tm, B_WIDTH), tok3),
                  pl.BlockSpec((1, C_WIDTH, mlen), bat3),
                  pl.BlockSpec((1, mlen, C_WIDTH), bat3),
                  vec, vec, wspec(wcq), wspec(wg), wspec(wm), wspec(wpa), wspec(wpb), wspec(wpc), wspec(wo),
                  vec, vec],
        out_specs=pl.BlockSpec((tm, d), tok),
        out_shape=jax.ShapeDtypeStruct((bsz * seq, d), F32),
        compiler_params=pltpu.CompilerParams(vmem_limit_bytes=VMEM_LIMIT),
        name="tail",
    )(x2, o_a, o_b, mkt, mv, lig, lib, wcq, wg, wm, wpa, wpb, wpc, wo, lg, lb)


def _rope_tables(seq):
    pos = jnp.arange(seq, dtype=F32)

    def cos_sin(dim):
        r = dim // ROPE_FRACTION
        half = r // 2
        inv = jnp.power(ROPE_THETA, -jnp.arange(half, dtype=F32) * (2.0 / r))
        ang = pos[:, None] * inv[None, :]
        return jnp.cos(ang), jnp.sin(ang), half

    c64, s64, h64 = cos_sin(A_HEAD_DIM)
    c32, s32, h32 = cos_sin(IDX_DIM)
    cq, sq = c64.T, s64.T
    ci = jnp.concatenate([c32.T, c32.T], axis=0)
    si = jnp.concatenate([-s32.T, s32.T], axis=0)

    def natural(cos, sin, half, dim, width):
        zero = jnp.zeros((seq, dim - 2 * half), F32)
        c = jnp.concatenate([cos, cos, jnp.ones_like(zero)], axis=1)
        a = jnp.concatenate([jnp.zeros_like(sin), sin, zero], axis=1)
        b = jnp.concatenate([-sin, jnp.zeros_like(sin), zero], axis=1)
        rep = width // dim
        return [jnp.tile(t, (1, rep)) for t in (c, a, b)]

    nk = jnp.stack(natural(c64, s64, h64, A_HEAD_DIM, LANES))
    return cq, sq, ci, si, nk


def kernel(x, mem, ln_in_g, ln_in_b, w_in, w_mem_kv, diff_lambda, diff_norm_g,
           w_proj_a, w_proj_b, w_proj_c, w_out, ln_g, ln_b):
    bsz, seq, d = x.shape
    assert d == D_MODEL and seq % QB == 0 and w_in.shape[0] == DEPTH == 1
    tm_proj = 512 if seq % 512 == 0 else QB
    tm_tail = 1024 if seq % 1024 == 0 else (512 if seq % 512 == 0 else _TAIL_ROWS)
    lam_init = 0.8 - 0.6 * math.exp(-0.3 * 0)

    offs = np.cumsum((0,) + IN_SPLITS)
    (r_aq, r_ak, r_av, r_ag, r_iq, r_ik, r_iw, r_bq, r_bk, r_bv, r_bg, r_cq, r_cg, r_mg) = [
        slice(offs[i], offs[i + 1]) for i in range(len(IN_SPLITS))]
    w_tb = lax.optimization_barrier(w_in[0].astype(BF16)).T
    f32_rows = slice(r_iq.start, r_bq.stop)
    w_t = {r_aq.start: w_in[0][:, r_aq].T, f32_rows.start: w_in[0][:, f32_rows].T}

    def rows_f32(r):
        base = r_aq.start if r.stop <= r_aq.stop else f32_rows.start
        return w_t[base][r.start - base:r.stop - base]
    wt = jnp.concatenate([(rows_f32(r_aq) * (A_HEAD_DIM ** -0.5 * LOG2E)).astype(BF16),
                          (rows_f32(r_bq) * (B_HEAD_DIM ** -0.5 * LOG2E)).astype(BF16),
                          w_tb[r_bv], w_tb[r_av]], axis=0)
    wx = jnp.concatenate([rows_f32(r_iq), rows_f32(r_ik), rows_f32(r_iw) * ((IDX_HEADS * IDX_DIM) ** -0.5),
                          jnp.zeros((_X_ROWS - _X_IW - IDX_HEADS, d), F32)], axis=0)
    wx_hi = wx.astype(BF16)
    wx_mid = (wx - wx_hi.astype(F32)).astype(BF16)
    wx_lo = (wx - wx_hi.astype(F32) - wx_mid.astype(F32)).astype(BF16)
    wx = jnp.concatenate([wx_hi, wx_mid, wx_lo], axis=0)
    wn = jnp.concatenate([w_tb[r_bk], w_tb[r_ak], jnp.zeros((LANES - A_HEAD_DIM, d), BF16)], axis=0).T
    wg = jnp.concatenate([w_tb[r_ag], w_tb[r_bg], w_tb[r_cg]], axis=0).T
    w_cq, w_mg = w_tb[r_cq].T, w_tb[r_mg].T
    row = lambda v: v.reshape(1, -1).astype(F32)

    x2 = x.reshape(bsz * seq, d)
    tabs = _rope_tables(seq)
    mkt, mv = _mem_kv(mem, w_mem_kv[0][:, :C_WIDTH].T.astype(BF16), w_mem_kv[0][:, C_WIDTH:].astype(BF16))
    aqt, bqt, iqt, iwt, avt, bvt, bk, ak, ik = _in_proj(
        x2, row(ln_in_g), row(ln_in_b), wn, wt, wx, tabs, bsz, seq, tm_proj)
    o_a, o_b = _mixers(aqt, iqt, iwt, ak.reshape(bsz, seq, A_HEAD_DIM), ik.reshape(bsz, seq, _IDX_K), avt,
                       bqt, bk.reshape(bsz, seq, B_WIDTH), bvt, diff_lambda[0].astype(F32),
                       diff_norm_g[0].reshape(-1, 1).astype(F32), lam_init, bsz, seq)
    out = _tail(x2, o_a, o_b, mkt, mv, row(ln_in_g), row(ln_in_b), w_cq, wg, w_mg,
                w_proj_a[0].astype(BF16), w_proj_b[0].astype(BF16), w_proj_c[0].astype(BF16),
                w_out[0].astype(BF16), row(ln_g[0]), row(ln_b[0]), bsz, seq, tm_tail)
    return out.reshape(bsz, seq, d)
```

```python
import functools
import itertools
import math

import numpy as np
import jax
import jax.numpy as jnp
from jax import lax
from jax.experimental import pallas as pl
from jax.experimental.pallas import tpu as pltpu

D_MODEL = 1024
CHUNK = 64
ROPE_THETA = 500000.0
ROPE_FRACTION = 4
LN_EPS = 1e-5
A_HEADS = 8
A_HEAD_DIM = 64
A_WIDTH = A_HEADS * A_HEAD_DIM
IDX_HEADS = 8
IDX_DIM = 32
TOPK_MAX = 256
B_HEADS = 4
B_HEAD_DIM = 64
B_WIDTH = B_HEADS * 2 * B_HEAD_DIM
C_HEADS = 4
C_HEAD_DIM = 128
C_WIDTH = C_HEADS * C_HEAD_DIM
N_BRANCH = 3
DEPTH = 1
DEEPNORM_ALPHA = (2.0 * DEPTH) ** 0.25
IN_SPLITS = (A_WIDTH, A_HEAD_DIM, A_HEAD_DIM, A_WIDTH, IDX_HEADS * IDX_DIM, IDX_DIM, IDX_HEADS,
             B_WIDTH, B_WIDTH, B_WIDTH, B_WIDTH, C_WIDTH, C_WIDTH, N_BRANCH * D_MODEL)

LANES = 128
QB = 256
KB = 256
NEG = -1e30
INT_MIN = -2 ** 31
VMEM_LIMIT = 56 * 1024 * 1024

BF16 = jnp.bfloat16
F32 = jnp.float32
I32 = jnp.int32
SUBLANES = 8
assert KB == 32 * SUBLANES


def _nt_dot(a, b):
    return lax.dot_general(a, b, (((1,), (1,)), ((), ())), preferred_element_type=F32)


def _dot(a, b):
    return jnp.dot(a, b, preferred_element_type=F32)


def _layer_norm(x, g, b):
    mu = jnp.mean(x, axis=-1, keepdims=True)
    xc = x - mu
    var = jnp.mean(xc * xc, axis=-1, keepdims=True)
    return xc * lax.rsqrt(var + LN_EPS) * g + b


def _mem_kv_kernel(mem_ref, wkt_ref, wv_ref, mkt_ref, mv_ref):
    nb, mlen, d = mem_ref.shape
    mb = mem_ref[...].reshape(nb * mlen, d).astype(BF16)
    kt = _nt_dot(wkt_ref[...], mb).astype(BF16)
    for i in range(nb):
        mkt_ref[i] = kt[:, i * mlen:(i + 1) * mlen]
    mv_ref[...] = _dot(mb, wv_ref[...]).astype(BF16).reshape(nb, mlen, C_WIDTH)


def _mem_kv(mem, wkt, wv):
    bsz, mlen, d = mem.shape
    nb = 4 if bsz % 4 == 0 else 1
    return pl.pallas_call(
        _mem_kv_kernel,
        grid=(bsz // nb,),
        in_specs=[pl.BlockSpec((nb, mlen, d), lambda b: (b, 0, 0)),
                  pl.BlockSpec((C_WIDTH, d), lambda b: (0, 0)),
                  pl.BlockSpec((d, C_WIDTH), lambda b: (0, 0))],
        out_specs=[pl.BlockSpec((nb, C_WIDTH, mlen), lambda b: (b, 0, 0)),
                   pl.BlockSpec((nb, mlen, C_WIDTH), lambda b: (b, 0, 0))],
        out_shape=[jax.ShapeDtypeStruct((bsz, C_WIDTH, mlen), BF16),
                   jax.ShapeDtypeStruct((bsz, mlen, C_WIDTH), BF16)],
        name="mem_kv",
    )(mem, wkt, wv)


_T_AQ = 0
_T_BQ = _T_AQ + A_WIDTH
_T_BV = _T_BQ + B_WIDTH
_T_AV = _T_BV + B_WIDTH
_T_AK = _T_AV + A_HEAD_DIM
_T_ROWS = _T_AK + A_HEAD_DIM
_X_IQ = 0
_X_IK = _X_IQ + IDX_HEADS * IDX_DIM
_X_IW = _X_IK + IDX_DIM
_X_ROWS = _X_IW + 16
_IDX_K = 256
_N_COLS = B_WIDTH
_ONES_ROWS = 16
_AV_ROWS = A_HEAD_DIM + _ONES_ROWS
_BV_ROWS = 2 * B_HEAD_DIM + _ONES_ROWS
LOG2E = math.log2(math.e)


def _rope_rows64(r, cos, sin):
    x1, x2 = r[0:8], r[8:16]
    return jnp.concatenate([x1 * cos - x2 * sin, x2 * cos + x1 * sin, r[16:]], axis=0)


def _rope_rows32(r, cos2, sin2):
    top = r[0:8]
    return jnp.concatenate([top * cos2 + pltpu.roll(top, 4, 0) * sin2, r[8:]], axis=0)


def _split3(x):
    hi = x.astype(BF16)
    r = x - hi.astype(F32)
    mid = r.astype(BF16)
    lo = (r - mid.astype(F32)).astype(BF16)
    return hi, mid, lo


def _in_proj_kernel(x_ref, g_ref, b_ref, wn_ref, wt_ref, wx_ref, cq_ref, sq_ref, ci_ref, si_ref, nk_ref,
                    aqt_ref, bqt_ref, iqt_ref, iwt_ref, avt_ref, bvt_ref, bk_ref, ak_ref, ik_ref, rt_scr, xt_scr):
    tm = x_ref.shape[0]
    h = _layer_norm(x_ref[...], g_ref[...], b_ref[...])
    h_hi, h_mid, h_lo = _split3(h)
    hb = h_hi

    kn = _dot(hb, wn_ref[...])
    c64, a64, b64 = nk_ref[0], nk_ref[1], nk_ref[2]
    for g in range(_N_COLS // LANES):
        v = kn[:, g * LANES:(g + 1) * LANES]
        v = (v * c64 + pltpu.roll(v, 8, 1) * a64 + pltpu.roll(v, LANES - 8, 1) * b64).astype(BF16)
        bk_ref[:, g * LANES:(g + 1) * LANES] = v

    n = _X_ROWS
    t1 = _nt_dot(wx_ref[...], h_hi)
    t2 = _nt_dot(wx_ref[0:2 * n, :], h_mid)
    t3 = _nt_dot(wx_ref[0:n, :], h_lo)
    xt_scr[...] = (t3 + t2[n:2 * n] + t1[2 * n:3 * n]) + (t2[0:n] + t1[n:2 * n]) + t1[0:n]
    ci, si = ci_ref[...], si_ref[...]
    zpad = jnp.zeros((_IDX_K - 6 * IDX_DIM, tm), BF16)
    for hd in range(IDX_HEADS):
        qh, qm, ql = _split3(_rope_rows32(xt_scr[_X_IQ + IDX_DIM * hd:_X_IQ + IDX_DIM * (hd + 1), :], ci, si))
        iqt_ref[0, _IDX_K * hd:_IDX_K * (hd + 1), :] = jnp.concatenate([qh, qm, ql, qh, qm, qh, zpad], axis=0)
    kh, km, kl = [t.astype(F32) for t in _split3(_rope_rows32(xt_scr[_X_IK:_X_IK + IDX_DIM, :], ci, si))]
    k6 = jnp.concatenate([kh, kh, kh, km, km, kl, zpad.astype(F32)], axis=0)
    ik_ref[...] = k6.T.astype(BF16)
    iwt_ref[0] = xt_scr[_X_IW:_X_IW + IDX_HEADS, :]

    rt_scr[...] = _nt_dot(wt_ref[...], hb)
    cos, sin = cq_ref[...], sq_ref[...]
    r = _rope_rows64(rt_scr[_T_AK:_T_AK + A_HEAD_DIM, :], cos, sin)
    r = jnp.concatenate([r, jnp.zeros((LANES - A_HEAD_DIM, tm), F32)], axis=0)
    ak_ref[...] = r.T[:, :A_HEAD_DIM].astype(BF16)
    for hd in range(A_HEADS):
        r = rt_scr[_T_AQ + 64 * hd:_T_AQ + 64 * (hd + 1), :]
        aqt_ref[0, 64 * hd:64 * (hd + 1), :] = _rope_rows64(r, cos, sin).astype(BF16)
    zeros = jnp.zeros((B_HEAD_DIM, tm), BF16)
    for mp in range(2 * B_HEADS):
        r = _rope_rows64(rt_scr[_T_BQ + 64 * mp:_T_BQ + 64 * (mp + 1), :], cos, sin).astype(BF16)
        bqt_ref[0, 128 * mp:128 * (mp + 1), :] = jnp.concatenate([r, zeros] if mp % 2 == 0 else [zeros, r], axis=0)
    ones = jnp.ones((_ONES_ROWS, KB), BF16)
    vdim = 2 * B_HEAD_DIM
    for t in range(tm // KB):
        cols = slice(t * KB, (t + 1) * KB)
        for hd in range(B_HEADS):
            bvt_ref[0, t, _BV_ROWS * hd:_BV_ROWS * hd + vdim, :] = (
                rt_scr[_T_BV + vdim * hd:_T_BV + vdim * (hd + 1), cols].astype(BF16))
            bvt_ref[0, t, _BV_ROWS * hd + vdim:_BV_ROWS * (hd + 1), :] = ones
        avt_ref[0, t, 0:A_HEAD_DIM, :] = rt_scr[_T_AV:_T_AV + A_HEAD_DIM, cols].astype(BF16)
        avt_ref[0, t, A_HEAD_DIM:_AV_ROWS, :] = ones


def _in_proj(x2, ln_g, ln_b, wn, wt, wx, tabs, bsz, seq, tm):
    d = x2.shape[1]
    spb = seq // tm
    tpk = tm // KB
    cq, sq, ci, si, nk = tabs
    tok = lambda i: (i, 0)
    tokt = lambda i: (i // spb, 0, i % spb)
    const2 = lambda i: (0, 0)
    return pl.pallas_call(
        _in_proj_kernel,
        grid=(bsz * spb,),
        in_specs=[pl.BlockSpec((tm, d), tok),
                  pl.BlockSpec((1, d), const2), pl.BlockSpec((1, d), const2),
                  pl.BlockSpec(wn.shape, const2), pl.BlockSpec(wt.shape, const2), pl.BlockSpec(wx.shape, const2),
                  pl.BlockSpec((8, tm), lambda i: (0, i % spb)), pl.BlockSpec((8, tm), lambda i: (0, i % spb)),
                  pl.BlockSpec((8, tm), lambda i: (0, i % spb)), pl.BlockSpec((8, tm), lambda i: (0, i % spb)),
                  pl.BlockSpec((3, tm, LANES), lambda i: (0, i % spb, 0))],
        out_specs=[pl.BlockSpec((1, A_WIDTH, tm), tokt),
                   pl.BlockSpec((1, 2 * B_WIDTH, tm), tokt),
                   pl.BlockSpec((1, IDX_HEADS * _IDX_K, tm), tokt),
                   pl.BlockSpec((1, IDX_HEADS, tm), tokt),
                   pl.BlockSpec((1, tpk, _AV_ROWS, KB), lambda i: (i // spb, i % spb, 0, 0)),
                   pl.BlockSpec((1, tpk, B_HEADS * _BV_ROWS, KB), lambda i: (i // spb, i % spb, 0, 0)),
                   pl.BlockSpec((tm, B_WIDTH), tok),
                   pl.BlockSpec((tm, A_HEAD_DIM), tok),
                   pl.BlockSpec((tm, _IDX_K), tok)],
        out_shape=[jax.ShapeDtypeStruct((bsz, A_WIDTH, seq), BF16),
                   jax.ShapeDtypeStruct((bsz, 2 * B_WIDTH, seq), BF16),
                   jax.ShapeDtypeStruct((bsz, IDX_HEADS * _IDX_K, seq), BF16),
                   jax.ShapeDtypeStruct((bsz, IDX_HEADS, seq), F32),
                   jax.ShapeDtypeStruct((bsz, seq // KB, _AV_ROWS, KB), BF16),
                   jax.ShapeDtypeStruct((bsz, seq // KB, B_HEADS * _BV_ROWS, KB), BF16),
                   jax.ShapeDtypeStruct((bsz * seq, B_WIDTH), BF16),
                   jax.ShapeDtypeStruct((bsz * seq, A_HEAD_DIM), BF16),
                   jax.ShapeDtypeStruct((bsz * seq, _IDX_K), BF16)],
        scratch_shapes=[pltpu.VMEM((_T_ROWS, tm), F32), pltpu.VMEM((_X_ROWS, tm), F32)],
        compiler_params=pltpu.CompilerParams(vmem_limit_bytes=VMEM_LIMIT),
        name="in_proj",
    )(x2, ln_g, ln_b, wn, wt, wx, cq, sq, ci, si, nk)


def _query_limit(j):
    lane = lax.broadcasted_iota(I32, (1, QB), 1)
    return j * QB + (lane // CHUNK + 1) * CHUNK


def _key_index(off):
    return off + lax.broadcasted_iota(I32, (KB, QB), 0)


def _bit_transpose32(words):
    a = list(words)
    j, m = 16, 0x0000FFFF
    while j:
        mask = m - (1 << 32) if m >= (1 << 31) else m
        for k in range(32):
            if k & j == 0:
                t = (a[k] ^ lax.shift_right_logical(a[k + j], jnp.int32(j))) & mask
                a[k] = a[k] ^ t
                a[k + j] = a[k + j] ^ lax.shift_left(t, jnp.int32(j))
        j >>= 1
        m = (m ^ (m << j)) & 0xFFFFFFFF
    return a


def _online_softmax(last, logits, mask, values, vrows, s_bufs, mx_bufs, m_scr, acc_scr, side=None,
                    block0_stored=False):
    def stage_logits(kb, buf, diag, stored=False):
        apply_mask = mask(kb, diag)
        maps = (s_bufs[buf][mp] for mp in range(m_scr.shape[0])) if stored else logits(kb)
        for mp, s in enumerate(maps):
            sb = apply_mask(s.astype(BF16))
            s_bufs[buf][mp] = sb
            mx_bufs[buf][mp, 0:1, :] = jnp.max(sb, axis=0, keepdims=True).astype(F32)
            yield

    def stage_values(kb, buf, is_last=False):
        steps = side(kb, is_last) if side is not None else iter(())
        for mp, vt in enumerate(values(kb)):
            next(steps, None)
            rows = slice(vrows * mp, vrows * (mp + 1))
            m_old = m_scr[mp, 0:1, :]
            m_new = jnp.maximum(m_old, mx_bufs[buf][mp, 0:1, :])
            alpha = jnp.exp2(m_old - m_new)
            p = jnp.exp2(s_bufs[buf][mp] - m_new.astype(BF16))
            acc_scr[rows, :] = alpha * acc_scr[rows, :] + _dot(vt, p)
            m_scr[mp, 0:1, :] = m_new
            yield
        for _ in steps:
            pass

    def run(*stages):
        end = object()
        live = list(stages)
        while live:
            if side is None:
                live = [g for g in live if next(g, end) is not end]
            else:
                for _ in live.pop(0):
                    pass

    def pair(t, carry):
        kb = 2 * t
        run(stage_logits(kb + 1, 1, False), stage_values(kb, 0))
        run(stage_logits(kb + 2, 0, False), stage_values(kb + 1, 1))
        return carry

    pl.when(last == 0)(lambda: run(stage_logits(0, 0, True, block0_stored)))
    pl.when(last > 0)(lambda: run(stage_logits(0, 0, False, block0_stored)))
    trips = jnp.maximum(last - 1, 0) // 2
    lax.fori_loop(0, trips, pair, 0)
    base = 2 * trips
    rem = last - base

    @pl.when(rem == 0)
    def _():
        run(stage_values(last, 0, True))

    @pl.when(rem == 1)
    def _():
        run(stage_logits(last, 1, True), stage_values(base, 0))
        run(stage_values(last, 1, True))

    @pl.when(rem == 2)
    def _():
        run(stage_logits(base + 1, 1, False), stage_values(base, 0))
        run(stage_logits(last, 0, True), stage_values(base + 1, 1))
        run(stage_values(last, 0, True))


def _mixers_kernel(n_sel, nqb, lam_init, aqt_ref, iqt_ref, iwt_ref, ak_ref, ik_ref, avt_ref,
                   bqt_ref, bk_ref, bvt_ref, dl_ref, g_ref, oa_ref, ob_ref,
                   keys_scr, planes_scr, thr_scr, s0_scr, s1_scr, mx0_scr, mx1_scr, m_scr, acc_scr):
    s_bufs, mx_bufs = (s0_scr, s1_scr), (mx0_scr, mx1_scr)
    j = pl.program_id(1)
    nkb = j + 1
    limit = _query_limit(j)
    iw = iwt_ref[0]

    def score_block(kb, diag):
        off = pl.multiple_of(kb * KB, KB)
        ikb = ik_ref[0, pl.ds(off, KB), :]
        sc = jnp.zeros((KB, QB), F32)
        for hd in range(IDX_HEADS):
            z = _dot(ikb, iqt_ref[0, _IDX_K * hd:_IDX_K * (hd + 1), :])
            sc = sc + iw[hd:hd + 1, :] * jnp.maximum(z, 0.0)
            yield
        bits = pltpu.bitcast(sc, I32)
        key = bits ^ ((bits >> 31) & 0x7FFFFFFF)
        if diag:
            key = jnp.where(_key_index(off) < limit, key, INT_MIN)
        keys_scr[pl.ds(off, KB), :] = key
        ukey = key ^ INT_MIN
        planes = _bit_transpose32([ukey[SUBLANES * i:SUBLANES * (i + 1)] for i in range(32)])
        for p in range(32):
            planes_scr[kb, p] = planes[p]

    vdim = 2 * B_HEAD_DIM
    nmaps = 2 * B_HEADS
    m_scr[...] = jnp.full(m_scr.shape, NEG, F32)
    acc_scr[...] = jnp.zeros(acc_scr.shape, F32)

    def keep_where(keep):
        keep16 = jnp.where(keep, 1, 0).astype(jnp.int16) != 0
        return lambda sb: jnp.where(keep16, sb, NEG)

    def b_logits(kb):
        off = pl.multiple_of(kb * KB, KB)
        for mp in range(nmaps):
            kpair = bk_ref[0, pl.ds(off, KB), vdim * (mp // 2):vdim * (mp // 2 + 1)]
            yield _dot(kpair, bqt_ref[0, vdim * mp:vdim * (mp + 1), :])

    def b_mask(kb, diag):
        if not diag:
            return lambda sb: sb
        return keep_where(_key_index(pl.multiple_of(kb * KB, KB)) < limit)

    def b_values(kb):
        return [bvt_ref[0, kb, _BV_ROWS * (mp // 2):_BV_ROWS * (mp // 2 + 1), :] for mp in range(nmaps)]

    _online_softmax(j, b_logits, b_mask, b_values, _BV_ROWS, s_bufs, mx_bufs, m_scr, acc_scr, side=score_block)

    def b_finish():
        dl = dl_ref[...]
        lam = (jnp.exp(jnp.sum(dl[0:1] * dl[1:2], axis=1, keepdims=True))
               - jnp.exp(jnp.sum(dl[2:3] * dl[3:4], axis=1, keepdims=True)) + lam_init)
        gain = g_ref[...] * (1.0 - lam_init)
        for hd in range(B_HEADS):
            b1, b2 = _BV_ROWS * 2 * hd, _BV_ROWS * (2 * hd + 1)
            o1 = acc_scr[b1:b1 + vdim, :] / acc_scr[b1 + vdim:b1 + vdim + 1, :]
            o2 = acc_scr[b2:b2 + vdim, :] / acc_scr[b2 + vdim:b2 + vdim + 1, :]
            o = o1 - lam * o2
            ms = jnp.mean(o * o, axis=0, keepdims=True)
            ob_ref[0, :, vdim * hd:vdim * (hd + 1)] = (o * lax.rsqrt(ms + LN_EPS) * gain).T.astype(BF16)
            yield

    def a_logits(kb):
        akb = ak_ref[0, pl.ds(pl.multiple_of(kb * KB, KB), KB), :]
        for hd in range(A_HEADS):
            yield _dot(akb, aqt_ref[0, A_HEAD_DIM * hd:A_HEAD_DIM * (hd + 1), :])

    def a_prefill():
        for hd, s in enumerate(a_logits(0)):
            s0_scr[hd] = s.astype(BF16)
            yield


    def select_threshold(nblk):
        def tree_sum(parts):
            while len(parts) > 1:
                parts = [a + b for a, b in zip(parts[0::2], parts[1::2])] + parts[len(parts) & ~1:]
            return jnp.sum(parts[0], axis=0, keepdims=True)

        def step(it, carry):
            need, thr_u = carry[0], carry[1]
            alive = carry[2:]
            ones = [alive[kb] & planes_scr[kb, it] for kb in range(nblk)]
            c1 = tree_sum([lax.population_count(x) for x in ones])
            take = c1 >= need
            need = jnp.where(take, need, need - c1)
            thr_u = thr_u | jnp.where(take, lax.shift_left(jnp.int32(1), 31 - it), 0)
            alive = [jnp.where(take, x, a ^ x) for x, a in zip(ones, alive)]
            return (need, thr_u, *alive)

        out = (jnp.full((1, QB), n_sel, I32), jnp.zeros((1, QB), I32),
               *([jnp.full((SUBLANES, QB), -1, I32)] * nblk))
        filler = itertools.chain(b_finish(), a_prefill())
        for it in range(32):
            next(filler, None)
            out = step(it, out)
        for _ in filler:
            pass
        need, thr_u = out[0], out[1]
        equal = tree_sum([lax.population_count(a) for a in out[2:]])
        thr_scr[0:1, :] = thr_u ^ INT_MIN
        thr_scr[1:2, :] = (n_sel - need) + equal

    for k in range(nqb):
        pl.when(j == k)(functools.partial(select_threshold, k + 1))
    thr = thr_scr[0:1, :]
    cnt = thr_scr[1:2, :]

    def count(pred_fn):
        def body(kb, c):
            off = pl.multiple_of(kb * KB, KB)
            hit = pred_fn(keys_scr[pl.ds(off, KB), :])
            return c + jnp.sum(jnp.where(hit, 1.0, 0.0), axis=0, keepdims=True)
        return lax.fori_loop(0, nkb, body, jnp.zeros((1, QB), F32))

    tied = (cnt > n_sel) & (thr > INT_MIN)

    @pl.when(jnp.max(jnp.where(tied, 1.0, 0.0)) > 0.0)
    def _():
        need = n_sel - count(lambda blk: blk > thr)
        thr_tied = jnp.where(tied, thr, INT_MIN)
        tri = (lax.broadcasted_iota(I32, (KB, KB), 0) >= lax.broadcasted_iota(I32, (KB, KB), 1)).astype(BF16)

        def body(kb, seen):
            off = pl.multiple_of(kb * KB, KB)
            blk = keys_scr[pl.ds(off, KB), :]
            eq = blk == thr_tied
            eqb = jnp.where(eq, 1.0, 0.0).astype(BF16)
            rank = _dot(tri, eqb) + seen
            keys_scr[pl.ds(off, KB), :] = jnp.where(eq & (rank > need), INT_MIN, blk)
            return seen + jnp.sum(eqb.astype(F32), axis=0, keepdims=True)

        lax.fori_loop(0, nkb, body, jnp.zeros((1, QB), F32))

    thr = jnp.maximum(thr, INT_MIN + 1)

    m_scr[...] = jnp.full(m_scr.shape, NEG, F32)
    acc_scr[0:A_HEADS * _AV_ROWS, :] = jnp.zeros((A_HEADS * _AV_ROWS, QB), F32)

    def a_mask(kb, diag):
        del diag
        return keep_where(keys_scr[pl.ds(pl.multiple_of(kb * KB, KB), KB), :] >= thr)

    def a_values(kb):
        avb = avt_ref[0, kb]
        return [avb] * A_HEADS

    _online_softmax(j, a_logits, a_mask, a_values, _AV_ROWS, s_bufs, mx_bufs, m_scr, acc_scr, block0_stored=True)

    outs = []
    for hd in range(A_HEADS):
        base = _AV_ROWS * hd
        outs.append(acc_scr[base:base + A_HEAD_DIM, :] / acc_scr[base + A_HEAD_DIM:base + A_HEAD_DIM + 1, :])
    oa_ref[0] = jnp.concatenate(outs, axis=0).T.astype(BF16)


def _mixers(aqt, iqt, iwt, ak, ik, avt, bqt, bk, bvt, dl, gcol, lam_init, bsz, seq):
    nqb = seq // QB
    nmaps = 2 * B_HEADS
    assert nmaps == A_HEADS
    qblk = lambda b, j: (b, 0, j)
    full3 = lambda b, j: (b, 0, 0)
    full4 = lambda b, j: (b, 0, 0, 0)
    const2 = lambda b, j: (0, 0)
    oblk = lambda b, j: (b, j, 0)
    n_sel = min(TOPK_MAX, seq // 4)
    return pl.pallas_call(
        functools.partial(_mixers_kernel, n_sel, nqb, lam_init),
        grid=(bsz, nqb),
        in_specs=[pl.BlockSpec((1, A_WIDTH, QB), qblk),
                  pl.BlockSpec((1, IDX_HEADS * _IDX_K, QB), qblk),
                  pl.BlockSpec((1, IDX_HEADS, QB), qblk),
                  pl.BlockSpec((1, seq, A_HEAD_DIM), full3),
                  pl.BlockSpec((1, seq, _IDX_K), full3),
                  pl.BlockSpec((1, seq // KB, _AV_ROWS, KB), full4),
                  pl.BlockSpec((1, 2 * B_WIDTH, QB), qblk),
                  pl.BlockSpec((1, seq, B_WIDTH), full3),
                  pl.BlockSpec((1, seq // KB, B_HEADS * _BV_ROWS, KB), full4),
                  pl.BlockSpec((4, B_HEAD_DIM), const2),
                  pl.BlockSpec((2 * B_HEAD_DIM, 1), const2)],
        out_specs=[pl.BlockSpec((1, QB, A_WIDTH), oblk), pl.BlockSpec((1, QB, B_WIDTH), oblk)],
        out_shape=[jax.ShapeDtypeStruct((bsz, seq, A_WIDTH), BF16), jax.ShapeDtypeStruct((bsz, seq, B_WIDTH), BF16)],
        scratch_shapes=[pltpu.VMEM((seq, QB), I32),
                        pltpu.VMEM((seq // KB, 32, SUBLANES, QB), I32),
                        pltpu.VMEM((8, QB), I32),
                        pltpu.VMEM((nmaps, KB, QB), BF16),
                        pltpu.VMEM((nmaps, KB, QB), BF16),
                        pltpu.VMEM((nmaps, 8, QB), F32),
                        pltpu.VMEM((nmaps, 8, QB), F32),
                        pltpu.VMEM((nmaps, 8, QB), F32),
                        pltpu.VMEM((nmaps * _BV_ROWS, QB), F32)],
        compiler_params=pltpu.CompilerParams(vmem_limit_bytes=VMEM_LIMIT),
        name="mixers",
    )(aqt, iqt, iwt, ak, ik, avt, bqt, bk, bvt, dl, gcol)


_TAIL_ROWS = 256


def _silu(x):
    return x * jax.nn.sigmoid(x)


def _tail_kernel(x_ref, oa_ref, ob_ref, mkt_ref, mv_ref, lig_ref, lib_ref, wcq_ref, wg_ref, wm_ref,
                 wpa_ref, wpb_ref, wpc_ref, wo_ref, lg_ref, lb_ref, out_ref):
    tm = x_ref.shape[0]
    groups = [slice(r, r + _TAIL_ROWS) for r in range(0, tm, _TAIL_ROWS)]
    each = lambda fn, *lists: [fn(*args) for args in zip(*lists)]

    h = [_layer_norm(x_ref[g, :], lig_ref[...], lib_ref[...]) for g in groups]
    hb = each(lambda v: v.astype(BF16), h)
    cq = each(lambda v: _dot(v, wcq_ref[...]).astype(BF16), hb)
    gate_cols = lambda n: slice(A_WIDTH * n, A_WIDTH * (n + 1))
    merge_cols = lambda n: slice(D_MODEL * n, D_MODEL * (n + 1))
    silu_gate = lambda n: each(lambda v: _silu(_dot(v, wg_ref[:, gate_cols(n)])), hb)
    merge_gate = lambda n: each(lambda v: jax.nn.sigmoid(_dot(v, wm_ref[:, merge_cols(n)])), hb)
    u_a = each(lambda g, sg: (oa_ref[0, g, :].astype(F32) * sg).astype(BF16), groups, silu_gate(0))
    u_b = each(lambda g, sg: (ob_ref[0, g, :].astype(F32) * sg).astype(BF16), groups, silu_gate(1))
    sg_c = silu_gate(2)

    heads = [slice(C_HEAD_DIM * hd, C_HEAD_DIM * (hd + 1)) for hd in range(C_HEADS)]

    def mem_probs(q):
        out = []
        for cols in heads:
            s = _dot(q[:, cols], mkt_ref[0, cols, :]) * (C_HEAD_DIM ** -0.5)
            p = jnp.exp(s - jnp.max(s, axis=-1, keepdims=True))
            out.append((p / jnp.sum(p, axis=-1, keepdims=True)).astype(BF16))
        return out

    probs = each(mem_probs, cq)
    mg_a = merge_gate(0)
    o_c = each(lambda ps: jnp.concatenate([_dot(p, mv_ref[0, :, cols]) for p, cols in zip(ps, heads)], axis=1),
               probs)
    merged = each(lambda u, mg: mg * _dot(u, wpa_ref[...]), u_a, mg_a)
    mg_b = merge_gate(1)
    merged = each(lambda m, u, mg: m + mg * _dot(u, wpb_ref[...]), merged, u_b, mg_b)
    mg_c = merge_gate(2)
    merged = each(lambda m, o, sg, mg: m + mg * _dot((o * sg).astype(BF16), wpc_ref[...]), merged, o_c, sg_c, mg_c)
    z = each(lambda hh, m: DEEPNORM_ALPHA * hh + _dot(m.astype(BF16), wo_ref[...]), h, merged)
    for g, zz in zip(groups, z):
        out_ref[g, :] = _layer_norm(zz, lg_ref[...], lb_ref[...])


def _tail(x2, o_a, o_b, mkt, mv, lig, lib, wcq, wg, wm, wpa, wpb, wpc, wo, lg, lb, bsz, seq, tm):
    d = x2.shape[1]
    spb = seq // tm
    mlen = mv.shape[1]
    tok = lambda i: (i, 0)
    tok3 = lambda i: (i // spb, i % spb, 0)
    bat3 = lambda i: (i // spb, 0, 0)
    const2 = lambda i: (0, 0)
    wspec = lambda w: pl.BlockSpec(w.shape, const2, pipeline_mode=pl.Buffered(1))
    vec = pl.BlockSpec((1, d), const2)
    return pl.pallas_call(
        _tail_kernel,
        grid=(bsz * spb,),
        in_specs=[pl.BlockSpec((tm, d), tok),
                  pl.BlockSpec((1, tm, A_WIDTH), tok3),
                  pl.BlockSpec((1, tm, B_WIDTH), tok3),
                  pl.BlockSpec((1, C_WIDTH, mlen), bat3),
                  pl.BlockSpec((1, mlen, C_WIDTH), bat3),
                  vec, vec, wspec(wcq), wspec(wg), wspec(wm), wspec(wpa), wspec(wpb), wspec(wpc), wspec(wo),
                  vec, vec],
        out_specs=pl.BlockSpec((tm, d), tok),
        out_shape=jax.ShapeDtypeStruct((bsz * seq, d), F32),
        compiler_params=pltpu.CompilerParams(vmem_limit_bytes=VMEM_LIMIT),
        name="tail",
    )(x2, o_a, o_b, mkt, mv, lig, lib, wcq, wg, wm, wpa, wpb, wpc, wo, lg, lb)


def _rope_tables(seq):
    pos = jnp.arange(seq, dtype=F32)

    def cos_sin(dim):
        r = dim // ROPE_FRACTION
        half = r // 2
        inv = jnp.power(ROPE_THETA, -jnp.arange(half, dtype=F32) * (2.0 / r))
        ang = pos[:, None] * inv[None, :]
        return jnp.cos(ang), jnp.sin(ang), half

    c64, s64, h64 = cos_sin(A_HEAD_DIM)
    c32, s32, h32 = cos_sin(IDX_DIM)
    cq, sq = c64.T, s64.T
    ci = jnp.concatenate([c32.T, c32.T], axis=0)
    si = jnp.concatenate([-s32.T, s32.T], axis=0)

    def natural(cos, sin, half, dim, width):
        zero = jnp.zeros((seq, dim - 2 * half), F32)
        c = jnp.concatenate([cos, cos, jnp.ones_like(zero)], axis=1)
        a = jnp.concatenate([jnp.zeros_like(sin), sin, zero], axis=1)
        b = jnp.concatenate([-sin, jnp.zeros_like(sin), zero], axis=1)
        rep = width // dim
        return [jnp.tile(t, (1, rep)) for t in (c, a, b)]

    nk = jnp.stack(natural(c64, s64, h64, A_HEAD_DIM, LANES))
    return cq, sq, ci, si, nk


def kernel(x, mem, ln_in_g, ln_in_b, w_in, w_mem_kv, diff_lambda, diff_norm_g,
           w_proj_a, w_proj_b, w_proj_c, w_out, ln_g, ln_b):
    bsz, seq, d = x.shape
    assert d == D_MODEL and seq % QB == 0 and w_in.shape[0] == DEPTH == 1
    tm_proj = 512 if seq % 512 == 0 else QB
    tm_tail = 1024 if seq % 1024 == 0 else (512 if seq % 512 == 0 else _TAIL_ROWS)
    lam_init = 0.8 - 0.6 * math.exp(-0.3 * 0)

    offs = np.cumsum((0,) + IN_SPLITS)
    w_t = w_in[0].T
    w_tb = w_t.astype(BF16)
    (r_aq, r_ak, r_av, r_ag, r_iq, r_ik, r_iw, r_bq, r_bk, r_bv, r_bg, r_cq, r_cg, r_mg) = [
        slice(offs[i], offs[i + 1]) for i in range(len(IN_SPLITS))]
    wt = jnp.concatenate([(w_t[r_aq] * (A_HEAD_DIM ** -0.5 * LOG2E)).astype(BF16),
                          (w_t[r_bq] * (B_HEAD_DIM ** -0.5 * LOG2E)).astype(BF16),
                          w_tb[r_bv], w_tb[r_av], w_tb[r_ak]], axis=0)
    wx = jnp.concatenate([w_t[r_iq], w_t[r_ik], w_t[r_iw] * ((IDX_HEADS * IDX_DIM) ** -0.5),
                          jnp.zeros((_X_ROWS - _X_IW - IDX_HEADS, d), F32)], axis=0)
    wx_hi = wx.astype(BF16)
    wx_mid = (wx - wx_hi.astype(F32)).astype(BF16)
    wx_lo = (wx - wx_hi.astype(F32) - wx_mid.astype(F32)).astype(BF16)
    wx = jnp.concatenate([wx_hi, wx_mid, wx_lo], axis=0)
    wn = w_tb[r_bk].T
    wg = jnp.concatenate([w_tb[r_ag], w_tb[r_bg], w_tb[r_cg]], axis=0).T
    w_cq, w_mg = w_tb[r_cq].T, w_tb[r_mg].T
    row = lambda v: v.reshape(1, -1).astype(F32)

    x2 = x.reshape(bsz * seq, d)
    tabs = _rope_tables(seq)
    mkt, mv = _mem_kv(mem, w_mem_kv[0][:, :C_WIDTH].T.astype(BF16), w_mem_kv[0][:, C_WIDTH:].astype(BF16))
    aqt, bqt, iqt, iwt, avt, bvt, bk, ak, ik = _in_proj(
        x2, row(ln_in_g), row(ln_in_b), wn, wt, wx, tabs, bsz, seq, tm_proj)
    o_a, o_b = _mixers(aqt, iqt, iwt, ak.reshape(bsz, seq, A_HEAD_DIM), ik.reshape(bsz, seq, _IDX_K), avt,
                       bqt, bk.reshape(bsz, seq, B_WIDTH), bvt, diff_lambda[0].astype(F32),
                       diff_norm_g[0].reshape(-1, 1).astype(F32), lam_init, bsz, seq)
    out = _tail(x2, o_a, o_b, mkt, mv, row(ln_in_g), row(ln_in_b), w_cq, wg, w_mg,
                w_proj_a[0].astype(BF16), w_proj_b[0].astype(BF16), w_proj_c[0].astype(BF16),
                w_out[0].astype(BF16), row(ln_g[0]), row(ln_b[0]), bsz, seq, tm_tail)
    return out.reshape(bsz, seq, d)
```
